```python
import jax
import jax.numpy as jnp
from jax import lax
import numpy as np

D_MODEL = 2048
BATCH = 4
SEQ = 2048
DEPTH = 4
DEC_BATCH = 2
DEC_SEQ = 8192
PAST_LEN = 128

D_MIX = D_MODEL
CONV_CH = D_MIX // 2
CONV_WIDTH = 31
CONV_PAD = CONV_WIDTH // 2
V_HEAD_DIM = 128
MLA_HEADS = (D_MIX - CONV_CH) // V_HEAD_DIM
QK_NOPE_DIM = 128
QK_ROPE_DIM = 64
Q_LORA_RANK = 512
KV_LORA_RANK = 256
ROPE_THETA = 10000.0
D_IN = 2 * CONV_CH + Q_LORA_RANK + KV_LORA_RANK + QK_ROPE_DIM
D_FF = 5632
N_MEM = 256
X_HEADS = 4
X_HEAD_DIM = D_MODEL // X_HEADS
Q_BLOCK = 128
EPS = 1e-6

kernel_name = 'hybrid_conformer_mla_macaron_encoder'


def rms_norm(x, g):
    x32 = x.astype(jnp.float32)
    y = x32 * lax.rsqrt(jnp.mean(x32 * x32, axis=-1, keepdims=True) + EPS)
    return y.astype(x.dtype) * g


def layer_norm(x, g, b):
    x32 = x.astype(jnp.float32)
    xc = x32 - jnp.mean(x32, axis=-1, keepdims=True)
    y = xc * lax.rsqrt(jnp.mean(xc * xc, axis=-1, keepdims=True) + EPS)
    return y.astype(x.dtype) * g + b


def swiglu(h, w_gate, w_up, w_down):
    return (jax.nn.silu(h @ w_gate) * (h @ w_up)) @ w_down


def rope_tables(seq, dtype):
    inv = 1.0 / (ROPE_THETA ** (jnp.arange(0, QK_ROPE_DIM, 2, dtype=jnp.float32) / QK_ROPE_DIM))
    ang = jnp.arange(seq, dtype=jnp.float32)[:, None] * inv[None, :]
    ang = jnp.concatenate([ang, ang], axis=-1)
    return jnp.cos(ang).astype(dtype), jnp.sin(ang).astype(dtype)


def apply_rope(x, cos, sin):
    x1, x2 = jnp.split(x, 2, axis=-1)
    return x * cos + jnp.concatenate([-x2, x1], axis=-1) * sin


def conv_module(u, dw_w, dw_b, ln_g, ln_b):
    a, b = jnp.split(u, 2, axis=-1)
    h = a * jax.nn.sigmoid(b)
    h = lax.conv_general_dilated(
        h, dw_w[:, None, :], window_strides=(1,), padding=[(CONV_PAD, CONV_PAD)],
        dimension_numbers=('NWC', 'WIO', 'NWC'), feature_group_count=CONV_CH) + dw_b
    return jax.nn.silu(layer_norm(h, ln_g, ln_b))


def mla_block_attention(q_nope, q_rope, k_nope, k_rope, v):
    bsz, seq = q_nope.shape[0], q_nope.shape[1]
    nblk = seq // Q_BLOCK
    scale = (QK_NOPE_DIM + QK_ROPE_DIM) ** -0.5

    def to_blocks(t):
        return jnp.moveaxis(t.reshape((bsz, nblk, Q_BLOCK) + t.shape[2:]), 1, 0)

    def one_block(qs):
        qn, qr = qs
        s = (jnp.einsum('bqhd,bkhd->bhqk', qn, k_nope, preferred_element_type=jnp.float32)
             + jnp.einsum('bqhr,bkr->bhqk', qr, k_rope, preferred_element_type=jnp.float32))
        p = jax.nn.softmax(s * scale, axis=-1).astype(v.dtype)
        return jnp.einsum('bhqk,bkhd->bqhd', p, v)

    out = lax.map(one_block, (to_blocks(q_nope), to_blocks(q_rope)))
    return jnp.moveaxis(out, 0, 1).reshape(bsz, seq, MLA_HEADS * V_HEAD_DIM)


def parallel_mixer(h, w_in, dw_w, dw_b, ln_g, ln_b, g_q_lat, g_kv_lat, w_q_up, w_kv_up,
                   g_group, w_out, cos, sin):
    bsz, seq, _ = h.shape
    u = h @ w_in
    o1 = 2 * CONV_CH
    o2 = o1 + Q_LORA_RANK
    o3 = o2 + KV_LORA_RANK
    conv_out = conv_module(u[..., :o1], dw_w, dw_b, ln_g, ln_b)
    q = (rms_norm(u[..., o1:o2], g_q_lat) @ w_q_up).reshape(
        bsz, seq, MLA_HEADS, QK_NOPE_DIM + QK_ROPE_DIM)
    kv = (rms_norm(u[..., o2:o3], g_kv_lat) @ w_kv_up).reshape(
        bsz, seq, MLA_HEADS, QK_NOPE_DIM + V_HEAD_DIM)
    q_nope = q[..., :QK_NOPE_DIM]
    q_rope = apply_rope(q[..., QK_NOPE_DIM:], cos[:, None, :], sin[:, None, :])
    k_nope = kv[..., :QK_NOPE_DIM]
    v = kv[..., QK_NOPE_DIM:]
    k_rope = apply_rope(u[..., o3:], cos, sin)
    attn_out = mla_block_attention(q_nope, q_rope, k_nope, k_rope, v)
    y = jnp.concatenate([rms_norm(conv_out, g_group[:CONV_CH]),
                         rms_norm(attn_out, g_group[CONV_CH:])], axis=-1)
    return y @ w_out


def memory_cross_attention(h, m, w_q, w_k, w_v, w_o):
    bsz, seq, _ = h.shape
    n_mem = m.shape[1]
    q = (h @ w_q).reshape(bsz, seq, X_HEADS, X_HEAD_DIM)
    k = (m @ w_k).reshape(bsz, n_mem, X_HEADS, X_HEAD_DIM)
    v = (m @ w_v).reshape(bsz, n_mem, X_HEADS, X_HEAD_DIM)
    s = jnp.einsum('bqhd,bkhd->bhqk', q, k, preferred_element_type=jnp.float32) * (X_HEAD_DIM ** -0.5)
    p = jax.nn.softmax(s, axis=-1).astype(v.dtype)
    o = jnp.einsum('bhqk,bkhd->bqhd', p, v).reshape(bsz, seq, X_HEADS * X_HEAD_DIM)
    return o @ w_o


def encoder_trunk(x, mem, p):
    cos, sin = rope_tables(x.shape[1], x.dtype)
    for l in range(DEPTH):
        h = rms_norm(x, p['g_ffn1_pre'][l])
        f = swiglu(h, p['w_ffn1_gate'][l], p['w_ffn1_up'][l], p['w_ffn1_down'][l])
        x = x + 0.5 * rms_norm(f, p['g_ffn1_post'][l])
        h = rms_norm(x, p['g_mix_pre'][l])
        t = parallel_mixer(h, p['w_in'][l], p['conv_dw_w'][l], p['conv_dw_b'][l],
                           p['conv_ln_g'][l], p['conv_ln_b'][l], p['g_q_lat'][l], p['g_kv_lat'][l],
                           p['w_q_up'][l], p['w_kv_up'][l], p['g_group_out'][l], p['w_out'][l],
                           cos, sin)
        x = x + rms_norm(t, p['g_mix_post'][l])
        h = rms_norm(x, p['g_x_pre'][l])
        m = rms_norm(mem, p['g_mem'][l])
        c = memory_cross_attention(h, m, p['w_xq'][l], p['w_xk'][l], p['w_xv'][l], p['w_xo'][l])
        x = x + rms_norm(c, p['g_x_post'][l])
        h = rms_norm(x, p['g_ffn2_pre'][l])
        f = swiglu(h, p['w_ffn2_gate'][l], p['w_ffn2_up'][l], p['w_ffn2_down'][l])
        x = x + 0.5 * rms_norm(f, p['g_ffn2_post'][l])
    return x


def setup_inputs(seed: int = 0) -> dict:
    key = jax.random.key(seed)
    keys = jax.random.split(key, 40)
    ctr = [0]

    def nk():
        ctr[0] += 1
        return keys[ctr[0] - 1]

    def nrm(shape, scale):
        return jax.random.normal(nk(), shape, jnp.float32) * scale

    def gain(n):
        return 1.0 + nrm((DEPTH, n), 0.02)

    d = {}
    d['x_prompt'] = nrm((BATCH, SEQ, D_MODEL), 1.0)
    d['x_sample'] = nrm((DEC_BATCH, DEC_SEQ, D_MODEL), 1.0)
    d['mem_prompt'] = nrm((BATCH, N_MEM, D_MODEL), 1.0)
    d['mem_sample'] = nrm((DEC_BATCH, N_MEM, D_MODEL), 1.0)
    d['g_ffn1_pre'] = gain(D_MODEL)
    d['g_ffn1_post'] = gain(D_MODEL)
    d['w_ffn1_gate'] = nrm((DEPTH, D_MODEL, D_FF), D_MODEL ** -0.5)
    d['w_ffn1_up'] = nrm((DEPTH, D_MODEL, D_FF), D_MODEL ** -0.5)
    d['w_ffn1_down'] = nrm((DEPTH, D_FF, D_MODEL), D_FF ** -0.5)
    d['g_mix_pre'] = gain(D_MODEL)
    d['g_mix_post'] = gain(D_MODEL)
    d['w_in'] = nrm((DEPTH, D_MODEL, D_IN), D_MODEL ** -0.5)
    d['conv_dw_w'] = nrm((DEPTH, CONV_WIDTH, CONV_CH), CONV_WIDTH ** -0.5)
    d['conv_dw_b'] = nrm((DEPTH, CONV_CH), 0.02)
    d['conv_ln_g'] = gain(CONV_CH)
    d['conv_ln_b'] = nrm((DEPTH, CONV_CH), 0.02)
    d['g_q_lat'] = gain(Q_LORA_RANK)
    d['g_kv_lat'] = gain(KV_LORA_RANK)
    d['w_q_up'] = nrm((DEPTH, Q_LORA_RANK, MLA_HEADS * (QK_NOPE_DIM + QK_ROPE_DIM)), Q_LORA_RANK ** -0.5)
    d['w_kv_up'] = nrm((DEPTH, KV_LORA_RANK, MLA_HEADS * (QK_NOPE_DIM + V_HEAD_DIM)), KV_LORA_RANK ** -0.5)
    d['g_group_out'] = gain(D_MIX)
    d['w_out'] = nrm((DEPTH, D_MIX, D_MODEL), D_MIX ** -0.5)
    d['g_x_pre'] = gain(D_MODEL)
    d['g_x_post'] = gain(D_MODEL)
    d['g_mem'] = gain(D_MODEL)
    d['w_xq'] = nrm((DEPTH, D_MODEL, D_MODEL), D_MODEL ** -0.5)
    d['w_xk'] = nrm((DEPTH, D_MODEL, D_MODEL), D_MODEL ** -0.5)
    d['w_xv'] = nrm((DEPTH, D_MODEL, D_MODEL), D_MODEL ** -0.5)
    d['w_xo'] = nrm((DEPTH, D_MODEL, D_MODEL), D_MODEL ** -0.5)
    d['g_ffn2_pre'] = gain(D_MODEL)
    d['g_ffn2_post'] = gain(D_MODEL)
    d['w_ffn2_gate'] = nrm((DEPTH, D_MODEL, D_FF), D_MODEL ** -0.5)
    d['w_ffn2_up'] = nrm((DEPTH, D_MODEL, D_FF), D_MODEL ** -0.5)
    d['w_ffn2_down'] = nrm((DEPTH, D_FF, D_MODEL), D_FF ** -0.5)
    return d


def reference(x_prompt, x_sample, mem_prompt, mem_sample,
              g_ffn1_pre, g_ffn1_post, w_ffn1_gate, w_ffn1_up, w_ffn1_down,
              g_mix_pre, g_mix_post, w_in, conv_dw_w, conv_dw_b, conv_ln_g, conv_ln_b,
              g_q_lat, g_kv_lat, w_q_up, w_kv_up, g_group_out, w_out,
              g_x_pre, g_x_post, g_mem, w_xq, w_xk, w_xv, w_xo,
              g_ffn2_pre, g_ffn2_post, w_ffn2_gate, w_ffn2_up, w_ffn2_down):
    p = dict(
        g_ffn1_pre=g_ffn1_pre, g_ffn1_post=g_ffn1_post, w_ffn1_gate=w_ffn1_gate,
        w_ffn1_up=w_ffn1_up, w_ffn1_down=w_ffn1_down,
        g_mix_pre=g_mix_pre, g_mix_post=g_mix_post, w_in=w_in, conv_dw_w=conv_dw_w,
        conv_dw_b=conv_dw_b, conv_ln_g=conv_ln_g, conv_ln_b=conv_ln_b,
        g_q_lat=g_q_lat, g_kv_lat=g_kv_lat, w_q_up=w_q_up, w_kv_up=w_kv_up,
        g_group_out=g_group_out, w_out=w_out,
        g_x_pre=g_x_pre, g_x_post=g_x_post, g_mem=g_mem, w_xq=w_xq, w_xk=w_xk, w_xv=w_xv, w_xo=w_xo,
        g_ffn2_pre=g_ffn2_pre, g_ffn2_post=g_ffn2_post, w_ffn2_gate=w_ffn2_gate,
        w_ffn2_up=w_ffn2_up, w_ffn2_down=w_ffn2_down)
    y_prompt = encoder_trunk(x_prompt, mem_prompt, p)
    y_sample = encoder_trunk(x_sample, mem_sample, p)
    return (y_prompt, y_sample)
```

```python
import functools

import jax
import jax.numpy as jnp
from jax import lax
from jax.experimental import pallas as pl
from jax.experimental.pallas import tpu as pltpu

EPS = 1e-6
ROPE_THETA = 10000.0
NOPE = 128
ROPE = 64
VDIM = 128
HEAD_W = 2 * NOPE
X_HEADS = 4
HALO = 16

BF = jnp.bfloat16
F32 = jnp.float32

VMEM_LIMIT = 56 * 1024 * 1024


def _cparams(*sem):
    return pltpu.CompilerParams(dimension_semantics=sem, vmem_limit_bytes=VMEM_LIMIT)


def _rms(x):
    return x * lax.rsqrt(jnp.mean(x * x, axis=-1, keepdims=True) + EPS)


def _dot(a, b):
    return jnp.dot(a, b, preferred_element_type=F32)


def _dot_t(a, b):
    return lax.dot_general(a, b, (((1,), (1,)), ((), ())), preferred_element_type=F32)


def _const_spec(shape, index_map):
    return pl.BlockSpec(shape, index_map, pipeline_mode=pl.Buffered(1))


def _ffn_body(x_ref, gpre_ref, wg_ref, wu_ref, wd_ref, gpost_ref, o_ref, h_ref, acc_ref):
    f = pl.program_id(1)

    @pl.when(f == 0)
    def _():
        h_ref[...] = (_rms(x_ref[...]) * gpre_ref[...]).astype(BF)

    h = h_ref[...]
    a = _dot(h, wg_ref[...])
    b = _dot(h, wu_ref[...])
    z = (a * jax.nn.sigmoid(a) * b).astype(BF)
    c = _dot(z, wd_ref[...])

    @pl.when(f == 0)
    def _():
        acc_ref[...] = c

    @pl.when(f > 0)
    def _():
        acc_ref[...] += c

    @pl.when(f == pl.num_programs(1) - 1)
    def _():
        o_ref[...] = x_ref[...] + 0.5 * (_rms(acc_ref[...]) * gpost_ref[...])


def _ffn(x, l, gpre, wg, wu, wd, gpost, *, tm, tf):
    t, d = x.shape
    dff = wg.shape[-1]
    return pl.pallas_call(
        _ffn_body,
        out_shape=jax.ShapeDtypeStruct((t, d), F32),
        grid=(t // tm, dff // tf),
        in_specs=[
            pl.BlockSpec((tm, d), lambda i, f: (i, 0)),
            pl.BlockSpec((None, 1, d), lambda i, f: (l, 0, 0)),
            pl.BlockSpec((None, d, tf), lambda i, f: (l, 0, f)),
            pl.BlockSpec((None, d, tf), lambda i, f: (l, 0, f)),
            pl.BlockSpec((None, tf, d), lambda i, f: (l, f, 0)),
            pl.BlockSpec((None, 1, d), lambda i, f: (l, 0, 0)),
        ],
        out_specs=pl.BlockSpec((tm, d), lambda i, f: (i, 0)),
        scratch_shapes=[pltpu.VMEM((tm, d), BF), pltpu.VMEM((tm, d), F32)],
        compiler_params=_cparams("parallel", "arbitrary"),
        name="ffn",
    )(x, gpre, wg, wu, wd, gpost)


def _mix_in_body(x_ref, gpre_ref, win_ref, gq_ref, gkv_ref, wqm_ref, wqr_ref, wk_ref, wv_ref,
                 cos_ref, sin_ref, glu_ref, q_ref, k_ref, v_ref, *, cc, ql, kvl, heads, scale):
    h = (_rms(x_ref[...]) * gpre_ref[...]).astype(BF)
    gc = min(cc, 512)
    for c in range(0, cc, gc):
        a = _dot(h, win_ref[:, c:c + gc])
        b = _dot(h, win_ref[:, cc + c:cc + c + gc])
        glu_ref[:, c:c + gc] = a * jax.nn.sigmoid(b)
    o1 = 2 * cc
    lat = _dot(h, win_ref[:, o1:])
    qn = (_rms(lat[:, :ql]) * gq_ref[...]).astype(BF)
    kvn = (_rms(lat[:, ql:ql + kvl]) * gkv_ref[...]).astype(BF)
    o3 = ql + kvl
    cosp = cos_ref[...]
    sinp = sin_ref[...]
    krope = (lat[:, o3:o3 + 128] * cosp + lat[:, o3 + 128:o3 + 256] * sinp).astype(BF)
    qm = _dot(qn, wqm_ref[...])
    qr = _dot(qn, wqr_ref[...])
    kn = _dot(kvn, wk_ref[...])
    v_ref[...] = _dot(kvn, wv_ref[...]).astype(BF)
    for hh in range(heads):
        b0 = hh * HEAD_W
        qrope = qm[:, b0:b0 + 128] * cosp + qr[:, hh * 128:(hh + 1) * 128] * sinp
        q_ref[:, b0:b0 + 128] = (qrope * scale).astype(BF)
        q_ref[:, b0 + 128:b0 + HEAD_W] = (qm[:, b0 + 128:b0 + HEAD_W] * scale).astype(BF)
        k_ref[:, b0:b0 + 128] = krope
        k_ref[:, b0 + 128:b0 + HEAD_W] = kn[:, hh * NOPE:(hh + 1) * NOPE].astype(BF)


def _mix_in(x, l, gpre, win, gq, gkv, wqm, wqr, wk, wv, cosp, sinp, *, seq, tm, cc, heads):
    t, d = x.shape
    ql, kvl = gq.shape[-1], gkv.shape[-1]
    nwin = win.shape[-1]
    spt = seq // tm
    body = functools.partial(_mix_in_body, cc=cc, ql=ql, kvl=kvl, heads=heads,
                             scale=float((NOPE + ROPE) ** -0.5))
    cst = lambda i: (l, 0, 0)
    return pl.pallas_call(
        body,
        out_shape=(jax.ShapeDtypeStruct((t, cc), F32),
                   jax.ShapeDtypeStruct((t, heads * HEAD_W), BF),
                   jax.ShapeDtypeStruct((t, heads * HEAD_W), BF),
                   jax.ShapeDtypeStruct((t, heads * VDIM), BF)),
        grid=(t // tm,),
        in_specs=[
            pl.BlockSpec((tm, d), lambda i: (i, 0)),
            _const_spec((None, 1, d), cst),
            _const_spec((None, d, nwin), cst),
            _const_spec((None, 1, ql), cst),
            _const_spec((None, 1, kvl), cst),
            _const_spec((None, ql, heads * HEAD_W), cst),
            _const_spec((None, ql, heads * 128), cst),
            _const_spec((None, kvl, heads * NOPE), cst),
            _const_spec((None, kvl, heads * VDIM), cst),
            pl.BlockSpec((tm, 128), lambda i: (i % spt, 0)),
            pl.BlockSpec((tm, 128), lambda i: (i % spt, 0)),
        ],
        out_specs=(pl.BlockSpec((tm, cc), lambda i: (i, 0)),
                   pl.BlockSpec((tm, heads * HEAD_W), lambda i: (i, 0)),
                   pl.BlockSpec((tm, heads * HEAD_W), lambda i: (i, 0)),
                   pl.BlockSpec((tm, heads * VDIM), lambda i: (i, 0))),
        compiler_params=_cparams("parallel"),
        name="mix_in",
    )(x, gpre, win, gq, gkv, wqm, wqr, wk, wv, cosp, sinp)


def _conv_body(x_ref, prev_ref, next_ref, w_ref, b_ref, lng_ref, lnb_ref, gg_ref, o_ref,
               win_ref, conv_ref, *, tm, kw, spt, rc):
    i = pl.program_id(0)
    first = (i % spt) == 0
    last = (i % spt) == spt - 1
    win_ref[0:HALO, :] = jnp.where(first, 0.0, prev_ref[...])
    win_ref[HALO:HALO + tm, :] = x_ref[...]
    win_ref[HALO + tm:, :] = jnp.where(last, 0.0, next_ref[...])
    cc = x_ref.shape[-1]
    shift = HALO - kw // 2
    for c in range(0, cc, 128):
        wk = [w_ref[k:k + 1, c:c + 128] for k in range(kw)]
        bias = b_ref[:, c:c + 128]
        for r in range(0, tm, rc):
            acc = wk[0] * win_ref[r + shift:r + shift + rc, c:c + 128]
            for k in range(1, kw):
                acc = acc + wk[k] * win_ref[r + shift + k:r + shift + k + rc, c:c + 128]
            conv_ref[r:r + rc, c:c + 128] = acc + bias
    hc = conv_ref[...]
    xc = hc - jnp.mean(hc, axis=-1, keepdims=True)
    y = xc * lax.rsqrt(jnp.mean(xc * xc, axis=-1, keepdims=True) + EPS) * lng_ref[...] + lnb_ref[...]
    s = y * jax.nn.sigmoid(y)
    o_ref[...] = (_rms(s) * gg_ref[...]).astype(BF)


def _conv(glu, l, w, b, lng, lnb, gg, *, seq, tm, rc):
    t, cc = glu.shape
    kw = w.shape[1]
    spt = seq // tm
    hb = tm // HALO
    nhb = t // HALO
    body = functools.partial(_conv_body, tm=tm, kw=kw, spt=spt, rc=rc)
    cst = lambda i: (l, 0, 0)
    return pl.pallas_call(
        body,
        out_shape=jax.ShapeDtypeStruct((t, cc), BF),
        grid=(t // tm,),
        in_specs=[
            pl.BlockSpec((tm, cc), lambda i: (i, 0)),
            pl.BlockSpec((HALO, cc), lambda i: (jnp.maximum(i * hb - 1, 0), 0)),
            pl.BlockSpec((HALO, cc), lambda i: (jnp.minimum((i + 1) * hb, nhb - 1), 0)),
            pl.BlockSpec((None, kw, cc), cst),
            pl.BlockSpec((None, 1, cc), cst),
            pl.BlockSpec((None, 1, cc), cst),
            pl.BlockSpec((None, 1, cc), cst),
            pl.BlockSpec((None, 1, cc), lambda i: (l, 0, 0)),
        ],
        out_specs=pl.BlockSpec((tm, cc), lambda i: (i, 0)),
        scratch_shapes=[pltpu.VMEM((tm + 2 * HALO, cc), F32), pltpu.VMEM((tm, cc), F32)],
        compiler_params=_cparams("parallel"),
        name="conv",
    )(glu, glu, glu, w, b, lng, lnb, gg)


def _attn_body(q_ref, k_ref, v_ref, o_ref, *, tk, nk):
    q = q_ref[...]
    tq = q.shape[0]

    def step(c, carry):
        m, lsum, acc = carry
        off = pl.multiple_of(c * tk, tk)
        s = _dot_t(q, k_ref[pl.ds(off, tk), :])
        m_new = jnp.maximum(m, jnp.max(s, axis=-1, keepdims=True))
        p = jnp.exp(s - m_new)
        alpha = jnp.exp(m - m_new)
        lsum = alpha * lsum + jnp.sum(p, axis=-1, keepdims=True)
        acc = alpha * acc + _dot(p.astype(BF), v_ref[pl.ds(off, tk), :])
        return m_new, lsum, acc

    init = (jnp.full((tq, 1), -jnp.inf, F32), jnp.zeros((tq, 1), F32), jnp.zeros((tq, VDIM), F32))
    _, lsum, acc = lax.fori_loop(0, nk, step, init)
    o_ref[...] = acc / lsum


def _attn(q, k, v, *, seq, heads, tq, tk):
    t = q.shape[0]
    bsz = t // seq
    qpt = seq // tq
    body = functools.partial(_attn_body, tk=tk, nk=seq // tk)
    return pl.pallas_call(
        body,
        out_shape=jax.ShapeDtypeStruct((t, heads * VDIM), F32),
        grid=(bsz, heads, qpt),
        in_specs=[
            pl.BlockSpec((tq, HEAD_W), lambda b, h, i: (b * qpt + i, h)),
            pl.BlockSpec((seq, HEAD_W), lambda b, h, i: (b, h)),
            pl.BlockSpec((seq, VDIM), lambda b, h, i: (b, h)),
        ],
        out_specs=pl.BlockSpec((tq, VDIM), lambda b, h, i: (b * qpt + i, h)),
        compiler_params=_cparams("parallel", "parallel", "arbitrary"),
        name="mla_attn",
    )(q, k, v)


def _proj_body(a_ref, b_ref, gb_ref, w_ref, gpost_ref, x_ref, o_ref, *, norm_b):
    half = a_ref.shape[-1]
    b = b_ref[...]
    if norm_b:
        b = (_rms(b) * gb_ref[...]).astype(BF)
    t = _dot(a_ref[...], w_ref[:half, :]) + _dot(b, w_ref[half:, :])
    o_ref[...] = x_ref[...] + _rms(t) * gpost_ref[...]


def _proj(a, b, a_col, b_col, l, gb, gb_col, w, gpost, x, *, tm, norm_b):
    t, d = x.shape
    half = w.shape[1] // 2
    body = functools.partial(_proj_body, norm_b=norm_b)
    cst = lambda i: (l, 0, 0)
    return pl.pallas_call(
        body,
        out_shape=jax.ShapeDtypeStruct((t, d), F32),
        grid=(t // tm,),
        in_specs=[
            pl.BlockSpec((tm, half), lambda i: (i, a_col)),
            pl.BlockSpec((tm, half), lambda i: (i, b_col)),
            pl.BlockSpec((None, 1, half), lambda i: (l, 0, gb_col)),
            _const_spec((None, 2 * half, d), cst),
            pl.BlockSpec((None, 1, d), cst),
            pl.BlockSpec((tm, d), lambda i: (i, 0)),
        ],
        out_specs=pl.BlockSpec((tm, d), lambda i: (i, 0)),
        compiler_params=_cparams("parallel"),
        name="proj",
    )(a, b, gb, w, gpost, x)


def _mem_kv_body(m_ref, g_ref, wk_ref, wv_ref, k_ref, v_ref):
    m = (_rms(m_ref[...]) * g_ref[...]).astype(BF)
    k_ref[...] = _dot(m, wk_ref[...]).astype(BF)
    v_ref[...] = _dot(m, wv_ref[...]).astype(BF)


def _mem_kv(mem, g, wk, wv, *, tn):
    rows, d = mem.shape
    depth = wk.shape[0]
    return pl.pallas_call(
        _mem_kv_body,
        out_shape=(jax.ShapeDtypeStruct((depth, rows, d), BF),
                   jax.ShapeDtypeStruct((depth, rows, d), BF)),
        grid=(depth, d // tn),
        in_specs=[
            pl.BlockSpec((rows, d), lambda l, n: (0, 0)),
            pl.BlockSpec((None, 1, d), lambda l, n: (l, 0, 0)),
            pl.BlockSpec((None, d, tn), lambda l, n: (l, 0, n)),
            pl.BlockSpec((None, d, tn), lambda l, n: (l, 0, n)),
        ],
        out_specs=(pl.BlockSpec((None, rows, tn), lambda l, n: (l, 0, n)),
                   pl.BlockSpec((None, rows, tn), lambda l, n: (l, 0, n))),
        compiler_params=_cparams("parallel", "parallel"),
        name="mem_kv",
    )(mem, g, wk, wv)


def _xattn_body(x_ref, g_ref, wq_ref, k_ref, v_ref, o_ref, *, scale):
    h = (_rms(x_ref[...]) * g_ref[...]).astype(BF)
    hd = x_ref.shape[-1] // X_HEADS
    for hh in range(X_HEADS):
        cs = slice(hh * hd, (hh + 1) * hd)
        qh = _dot(h, wq_ref[:, cs]).astype(BF)
        s = _dot_t(qh, k_ref[:, cs]) * scale
        e = jnp.exp(s - jnp.max(s, axis=-1, keepdims=True))
        p = e * (1.0 / jnp.sum(e, axis=-1, keepdims=True))
        o_ref[:, cs] = _dot(p.astype(BF), v_ref[:, cs]).astype(BF)


def _xattn(x, l, g, wq, kmem, vmem, *, seq, n_mem, tm):
    t, d = x.shape
    spt = seq // tm
    body = functools.partial(_xattn_body, scale=float((d // X_HEADS) ** -0.5))
    cst = lambda i: (l, 0, 0)
    return pl.pallas_call(
        body,
        out_shape=jax.ShapeDtypeStruct((t, d), BF),
        grid=(t // tm,),
        in_specs=[
            pl.BlockSpec((tm, d), lambda i: (i, 0)),
            pl.BlockSpec((None, 1, d), cst),
            _const_spec((None, d, d), cst),
            pl.BlockSpec((None, n_mem, d), lambda i: (l, i // spt, 0)),
            pl.BlockSpec((None, n_mem, d), lambda i: (l, i // spt, 0)),
        ],
        out_specs=pl.BlockSpec((tm, d), lambda i: (i, 0)),
        compiler_params=_cparams("parallel"),
        name="xattn",
    )(x, g, wq, kmem, vmem)


def _rope_tables(seq):
    inv = 1.0 / (ROPE_THETA ** (jnp.arange(0, ROPE, 2, dtype=F32) / ROPE))
    ang = jnp.arange(seq, dtype=F32)[:, None] * inv[None, :]
    ang = jnp.concatenate([ang, ang], axis=-1)
    pad = jnp.zeros((seq, 128 - ROPE), F32)
    return (jnp.concatenate([jnp.cos(ang), pad], axis=-1),
            jnp.concatenate([jnp.sin(ang), pad], axis=-1))


def _rot_cols(w):
    half = w.shape[-1] // 2
    return jnp.concatenate([-w[..., half:], w[..., :half]], axis=-1)


def _tile(n, want):
    t = min(n, want)
    while n % t:
        t //= 2
    return t


def _trunk(x3, mem3, p, depth, heads, cc):
    bsz, seq, d = x3.shape
    n_mem = mem3.shape[1]
    x = x3.reshape(bsz * seq, d)
    mem = mem3.reshape(bsz * n_mem, d)
    cosp, sinp = _rope_tables(seq)
    kmem, vmem = _mem_kv(mem, p['g_mem'], p['w_xk'], p['w_xv'], tn=_tile(d, 512))
    tm = _tile(seq, 512)
    tm_mix = _tile(seq, 256)
    tm_conv = _tile(seq, 128)
    dff = p['w_ffn1_gate'].shape[-1]
    tf = _tile(dff, 512)
    for l in range(depth):
        x = _ffn(x, l, p['g_ffn1_pre'], p['w_ffn1_gate'], p['w_ffn1_up'], p['w_ffn1_down'],
                 p['g_ffn1_post'], tm=tm, tf=tf)
        glu, q, k, v = _mix_in(x, l, p['g_mix_pre'], p['w_in_ext'], p['g_q_lat'], p['g_kv_lat'],
                               p['w_qm'], p['w_qr'], p['w_k'], p['w_v'], cosp, sinp,
                               seq=seq, tm=tm_mix, cc=cc, heads=heads)
        conv_n = _conv(glu, l, p['conv_dw_w'], p['conv_dw_b'], p['conv_ln_g'], p['conv_ln_b'],
                       p['g_group_out'], seq=seq, tm=tm_conv, rc=_tile(tm_conv, 64))
        attn = _attn(q, k, v, seq=seq, heads=heads, tq=_tile(seq, 256), tk=_tile(seq, 512))
        x = _proj(conv_n, attn, 0, 0, l, p['g_group_out'], 1, p['w_out'], p['g_mix_post'], x,
                  tm=tm, norm_b=True)
        o = _xattn(x, l, p['g_x_pre'], p['w_xq'], kmem, vmem, seq=seq, n_mem=n_mem, tm=tm)
        x = _proj(o, o, 0, 1, l, p['g_group_out'], 0, p['w_xo'], p['g_x_post'], x,
                  tm=tm, norm_b=False)
        x = _ffn(x, l, p['g_ffn2_pre'], p['w_ffn2_gate'], p['w_ffn2_up'], p['w_ffn2_down'],
                 p['g_ffn2_post'], tm=tm, tf=tf)
    return x.reshape(bsz, seq, d)


def kernel(x_prompt, x_sample, mem_prompt, mem_sample, g_ffn1_pre, g_ffn1_post, w_ffn1_gate, w_ffn1_up, w_ffn1_down, g_mix_pre, g_mix_post, w_in, conv_dw_w, conv_dw_b, conv_ln_g, conv_ln_b, g_q_lat, g_kv_lat, w_q_up, w_kv_up, g_group_out, w_out, g_x_pre, g_x_post, g_mem, w_xq, w_xk, w_xv, w_xo, g_ffn2_pre, g_ffn2_post, w_ffn2_gate, w_ffn2_up, w_ffn2_down):
    depth, d, _ = w_in.shape
    cc = conv_dw_w.shape[-1]
    ql, kvl = g_q_lat.shape[-1], g_kv_lat.shape[-1]
    heads = w_kv_up.shape[-1] // (NOPE + VDIM)
    o3 = 2 * cc + ql + kvl

    row = lambda g: g[:, None, :]
    p = dict(
        g_ffn1_pre=row(g_ffn1_pre), g_ffn1_post=row(g_ffn1_post),
        g_ffn2_pre=row(g_ffn2_pre), g_ffn2_post=row(g_ffn2_post),
        g_mix_pre=row(g_mix_pre), g_mix_post=row(g_mix_post),
        g_x_pre=row(g_x_pre), g_x_post=row(g_x_post), g_mem=row(g_mem),
        g_q_lat=row(g_q_lat), g_kv_lat=row(g_kv_lat), g_group_out=row(g_group_out),
        conv_dw_w=conv_dw_w, conv_dw_b=row(conv_dw_b), conv_ln_g=row(conv_ln_g),
        conv_ln_b=row(conv_ln_b),
        w_ffn1_gate=w_ffn1_gate.astype(BF), w_ffn1_up=w_ffn1_up.astype(BF),
        w_ffn1_down=w_ffn1_down.astype(BF),
        w_ffn2_gate=w_ffn2_gate.astype(BF), w_ffn2_up=w_ffn2_up.astype(BF),
        w_ffn2_down=w_ffn2_down.astype(BF),
        w_out=w_out.astype(BF), w_xq=w_xq.astype(BF), w_xk=w_xk.astype(BF),
        w_xv=w_xv.astype(BF), w_xo=w_xo.astype(BF),
    )
    w_kr = w_in[:, :, o3:]
    z64 = jnp.zeros((depth, d, 128 - ROPE), w_in.dtype)
    p['w_in_ext'] = jnp.concatenate([w_in[:, :, :o3], w_kr, z64, _rot_cols(w_kr), z64],
                                    axis=-1).astype(BF)
    wq4 = w_q_up.reshape(depth, ql, heads, NOPE + ROPE)
    wq_nope, wq_rope = wq4[..., :NOPE], wq4[..., NOPE:]
    zq = jnp.zeros((depth, ql, heads, 128 - ROPE), w_q_up.dtype)
    p['w_qm'] = jnp.concatenate([wq_rope, zq, wq_nope], axis=-1).reshape(
        depth, ql, heads * HEAD_W).astype(BF)
    p['w_qr'] = jnp.concatenate([_rot_cols(wq_rope), zq], axis=-1).reshape(
        depth, ql, heads * 128).astype(BF)
    wkv4 = w_kv_up.reshape(depth, kvl, heads, NOPE + VDIM)
    p['w_k'] = wkv4[..., :NOPE].reshape(depth, kvl, heads * NOPE).astype(BF)
    p['w_v'] = wkv4[..., NOPE:].reshape(depth, kvl, heads * VDIM).astype(BF)

    y_prompt = _trunk(x_prompt, mem_prompt, p, depth, heads, cc)
    y_sample = _trunk(x_sample, mem_sample, p, depth, heads, cc)
    return (y_prompt, y_sample)
```

```python
import functools

import jax
import jax.numpy as jnp
from jax import lax
from jax.experimental import pallas as pl
from jax.experimental.pallas import tpu as pltpu

EPS = 1e-6
LOG2_E = 1.4426950408889634
ROPE_THETA = 10000.0
NOPE = 128
ROPE = 64
VDIM = 128
HEAD_W = 2 * NOPE
X_HEADS = 4
HALO = 16
SUBLANES = 8

BF = jnp.bfloat16
F32 = jnp.float32

VMEM_LIMIT = 56 * 1024 * 1024


def _cparams(*sem):
    return pltpu.CompilerParams(dimension_semantics=sem, vmem_limit_bytes=VMEM_LIMIT)


def _rms(x):
    return x * lax.rsqrt(jnp.mean(x * x, axis=-1, keepdims=True) + EPS)


def _dot(a, b):
    return jnp.dot(a, b, preferred_element_type=F32)


def _dot_t(a, b):
    return lax.dot_general(a, b, (((1,), (1,)), ((), ())), preferred_element_type=F32)


def _const_spec(shape, index_map):
    return pl.BlockSpec(shape, index_map, pipeline_mode=pl.Buffered(1))


def _ffn_body(x_ref, gpre_ref, wg_ref, wu_ref, wd_ref, gpost_ref, o_ref, h_ref, acc_ref):
    f = pl.program_id(1)

    @pl.when(f == 0)
    def _():
        h_ref[...] = (_rms(x_ref[...]) * gpre_ref[...]).astype(BF)
        acc_ref[...] = jnp.zeros_like(acc_ref)

    h = h_ref[...]
    a = _dot(h, wg_ref[...])
    b = _dot(h, wu_ref[...])
    z = (a * jax.nn.sigmoid(a) * b).astype(BF)
    acc_ref[...] += _dot(z, wd_ref[...])

    @pl.when(f == pl.num_programs(1) - 1)
    def _():
        o_ref[...] = x_ref[...] + 0.5 * (_rms(acc_ref[...]) * gpost_ref[...])


def _ffn(x, l, gpre, wg, wu, wd, gpost, *, tm, tf):
    t, d = x.shape
    dff = wg.shape[-1]
    return pl.pallas_call(
        _ffn_body,
        out_shape=jax.ShapeDtypeStruct((t, d), F32),
        grid=(t // tm, dff // tf),
        in_specs=[
            pl.BlockSpec((tm, d), lambda i, f: (i, 0)),
            pl.BlockSpec((None, 1, d), lambda i, f: (l, 0, 0)),
            pl.BlockSpec((None, d, tf), lambda i, f: (l, 0, f)),
            pl.BlockSpec((None, d, tf), lambda i, f: (l, 0, f)),
            pl.BlockSpec((None, tf, d), lambda i, f: (l, f, 0)),
            pl.BlockSpec((None, 1, d), lambda i, f: (l, 0, 0)),
        ],
        out_specs=pl.BlockSpec((tm, d), lambda i, f: (i, 0)),
        scratch_shapes=[pltpu.VMEM((tm, d), BF), pltpu.VMEM((tm, d), F32)],
        compiler_params=_cparams("parallel", "arbitrary"),
        name="ffn",
    )(x, gpre, wg, wu, wd, gpost)


def _mix_in_body(x_ref, gpre_ref, win_ref, gq_ref, gkv_ref, wqm_ref, wqr_ref, wk_ref, wv_ref,
                 cos_ref, sin_ref, glu_ref, qt_ref, k_ref, vt_ref, *, cc, ql, kvl, heads, scale):
    h = (_rms(x_ref[...]) * gpre_ref[...]).astype(BF)
    gc = min(cc, 512)
    for c in range(0, cc, gc):
        a = _dot(h, win_ref[:, c:c + gc])
        b = _dot(h, win_ref[:, cc + c:cc + c + gc])
        glu_ref[:, c:c + gc] = a * jax.nn.sigmoid(b)
    o1 = 2 * cc
    lat = _dot(h, win_ref[:, o1:])
    qn = (_rms(lat[:, :ql]) * gq_ref[...]).astype(BF)
    kvn = (_rms(lat[:, ql:ql + kvl]) * gkv_ref[...]).astype(BF)
    o3 = ql + kvl
    cosp = cos_ref[...]
    sinp = sin_ref[...]
    krope = (lat[:, o3:o3 + 128] * cosp + lat[:, o3 + 128:o3 + 256] * sinp).astype(BF)
    qm = _dot(qn, wqm_ref[...])
    qr = _dot(qn, wqr_ref[...])
    kn = _dot(kvn, wk_ref[...])
    vt_ref[...] = _dot(kvn, wv_ref[...]).T.astype(BF)
    for hh in range(heads):
        b0 = hh * HEAD_W
        qrope = qm[:, b0:b0 + 128] * cosp + qr[:, hh * 128:(hh + 1) * 128] * sinp
        qt_ref[b0:b0 + 128, :] = (qrope * scale).T.astype(BF)
        qt_ref[b0 + 128:b0 + HEAD_W, :] = (qm[:, b0 + 128:b0 + HEAD_W] * scale).T.astype(BF)
        k_ref[:, b0:b0 + 128] = krope
        k_ref[:, b0 + 128:b0 + HEAD_W] = kn[:, hh * NOPE:(hh + 1) * NOPE].astype(BF)


def _mix_in(x, l, gpre, win, gq, gkv, wqm, wqr, wk, wv, cosp, sinp, *, seq, tm, cc, heads):
    t, d = x.shape
    ql, kvl = gq.shape[-1], gkv.shape[-1]
    nwin = win.shape[-1]
    spt = seq // tm
    body = functools.partial(_mix_in_body, cc=cc, ql=ql, kvl=kvl, heads=heads,
                             scale=float((NOPE + ROPE) ** -0.5 * LOG2_E))
    cst = lambda i: (l, 0, 0)
    return pl.pallas_call(
        body,
        out_shape=(jax.ShapeDtypeStruct((t, cc), F32),
                   jax.ShapeDtypeStruct((heads * HEAD_W, t), BF),
                   jax.ShapeDtypeStruct((t, heads * HEAD_W), BF),
                   jax.ShapeDtypeStruct((heads * VDIM, t), BF)),
        grid=(t // tm,),
        in_specs=[
            pl.BlockSpec((tm, d), lambda i: (i, 0)),
            _const_spec((None, 1, d), cst),
            _const_spec((None, d, nwin), cst),
            _const_spec((None, 1, ql), cst),
            _const_spec((None, 1, kvl), cst),
            _const_spec((None, ql, heads * HEAD_W), cst),
            _const_spec((None, ql, heads * 128), cst),
            _const_spec((None, kvl, heads * NOPE), cst),
            _const_spec((None, kvl, heads * VDIM), cst),
            pl.BlockSpec((tm, 128), lambda i: (i % spt, 0)),
            pl.BlockSpec((tm, 128), lambda i: (i % spt, 0)),
        ],
        out_specs=(pl.BlockSpec((tm, cc), lambda i: (i, 0)),
                   pl.BlockSpec((heads * HEAD_W, tm), lambda i: (0, i)),
                   pl.BlockSpec((tm, heads * HEAD_W), lambda i: (i, 0)),
                   pl.BlockSpec((heads * VDIM, tm), lambda i: (0, i))),
        compiler_params=_cparams("parallel"),
        name="mix_in",
    )(x, gpre, win, gq, gkv, wqm, wqr, wk, wv, cosp, sinp)


def _conv_body(x_ref, prev_ref, next_ref, w_ref, b_ref, lng_ref, lnb_ref, gg_ref, o_ref,
               win_ref, sh_ref, conv_ref, *, tm, kw, spt, rc):
    i = pl.program_id(0)
    first = (i % spt) == 0
    last = (i % spt) == spt - 1
    win_ref[0:HALO, :] = jnp.where(first, 0.0, prev_ref[...])
    win_ref[HALO:HALO + tm, :] = x_ref[...]
    win_ref[HALO + tm:, :] = jnp.where(last, 0.0, next_ref[...])
    cc = x_ref.shape[-1]
    shift = HALO - kw // 2
    nsh = sh_ref.shape[1]
    for b in range(1, SUBLANES):
        sh_ref[b - 1] = win_ref[b:b + nsh, :]

    def tap(k, r, c):
        off = shift + k
        a, b = off // SUBLANES, off % SUBLANES
        r0 = r + a * SUBLANES
        if b == 0:
            return win_ref[r0:r0 + rc, c:c + 128]
        return sh_ref[b - 1, r0:r0 + rc, c:c + 128]

    for c in range(0, cc, 128):
        wk = [w_ref[k:k + 1, c:c + 128] for k in range(kw)]
        bias = b_ref[:, c:c + 128]
        for r in range(0, tm, rc):
            acc = wk[0] * tap(0, r, c)
            for k in range(1, kw):
                acc = acc + wk[k] * tap(k, r, c)
            conv_ref[r:r + rc, c:c + 128] = acc + bias
    hc = conv_ref[...]
    xc = hc - jnp.mean(hc, axis=-1, keepdims=True)
    y = xc * lax.rsqrt(jnp.mean(xc * xc, axis=-1, keepdims=True) + EPS) * lng_ref[...] + lnb_ref[...]
    s = y * jax.nn.sigmoid(y)
    o_ref[...] = (_rms(s) * gg_ref[...]).astype(BF)


def _conv(glu, l, w, b, lng, lnb, gg, *, seq, tm, rc):
    t, cc = glu.shape
    kw = w.shape[1]
    spt = seq // tm
    hb = tm // HALO
    nhb = t // HALO
    nsh = tm + SUBLANES * ((HALO - kw // 2 + kw - 1) // SUBLANES)
    assert kw // 2 <= HALO and nsh + SUBLANES - 1 <= tm + 2 * HALO
    body = functools.partial(_conv_body, tm=tm, kw=kw, spt=spt, rc=rc)
    cst = lambda i: (l, 0, 0)
    return pl.pallas_call(
        body,
        out_shape=jax.ShapeDtypeStruct((t, cc), BF),
        grid=(t // tm,),
        in_specs=[
            pl.BlockSpec((tm, cc), lambda i: (i, 0)),
            pl.BlockSpec((HALO, cc), lambda i: (jnp.maximum(i * hb - 1, 0), 0)),
            pl.BlockSpec((HALO, cc), lambda i: (jnp.minimum((i + 1) * hb, nhb - 1), 0)),
            pl.BlockSpec((None, kw, cc), cst),
            pl.BlockSpec((None, 1, cc), cst),
            pl.BlockSpec((None, 1, cc), cst),
            pl.BlockSpec((None, 1, cc), cst),
            pl.BlockSpec((None, 1, cc), lambda i: (l, 0, 0)),
        ],
        out_specs=pl.BlockSpec((tm, cc), lambda i: (i, 0)),
        scratch_shapes=[pltpu.VMEM((tm + 2 * HALO, cc), F32),
                        pltpu.VMEM((SUBLANES - 1, nsh, cc), F32),
                        pltpu.VMEM((tm, cc), F32)],
        compiler_params=_cparams("parallel"),
        name="conv",
    )(glu, glu, glu, w, b, lng, lnb, gg)


def _attn_body(qt_ref, k_ref, vt_ref, o_ref, *, tk, nk):
    qt = qt_ref[...]
    tq = qt.shape[1]
    m = jnp.full((1, tq), -jnp.inf, F32)
    lsum = jnp.zeros((1, tq), F32)
    acc = jnp.zeros((VDIM, tq), F32)
    st_next = _dot(k_ref[0:tk, :], qt)
    for c in range(nk):
        st = st_next
        if c + 1 < nk:
            st_next = _dot(k_ref[(c + 1) * tk:(c + 2) * tk, :], qt)
        m_new = jnp.maximum(m, jnp.max(st, axis=0, keepdims=True))
        p = jnp.exp2(st - m_new)
        alpha = jnp.exp2(m - m_new)
        lsum = alpha * lsum + jnp.sum(p, axis=0, keepdims=True)
        acc = alpha * acc + _dot(vt_ref[:, c * tk:(c + 1) * tk], p.astype(BF))
        m = m_new
    o_ref[...] = (acc / lsum).T


def _attn(qt, k, vt, *, seq, heads, tq, tk):
    t = k.shape[0]
    bsz = t // seq
    qpt = seq // tq
    body = functools.partial(_attn_body, tk=tk, nk=seq // tk)
    return pl.pallas_call(
        body,
        out_shape=jax.ShapeDtypeStruct((t, heads * VDIM), F32),
        grid=(bsz, heads, qpt),
        in_specs=[
            pl.BlockSpec((HEAD_W, tq), lambda b, h, i: (h, b * qpt + i)),
            pl.BlockSpec((seq, HEAD_W), lambda b, h, i: (b, h)),
            pl.BlockSpec((VDIM, seq), lambda b, h, i: (h, b)),
        ],
        out_specs=pl.BlockSpec((tq, VDIM), lambda b, h, i: (b * qpt + i, h)),
        compiler_params=_cparams("parallel", "parallel", "arbitrary"),
        name="mla_attn",
    )(qt, k, vt)


def _proj_body(a_ref, b_ref, gb_ref, w_ref, gpost_ref, x_ref, o_ref, *, norm_b):
    half = a_ref.shape[-1]
    b = b_ref[...]
    if norm_b:
        b = (_rms(b) * gb_ref[...]).astype(BF)
    t = _dot(a_ref[...], w_ref[:half, :]) + _dot(b, w_ref[half:, :])
    o_ref[...] = x_ref[...] + _rms(t) * gpost_ref[...]


def _proj(a, b, a_col, b_col, l, gb, gb_col, w, gpost, x, *, tm, norm_b):
    t, d = x.shape
    half = w.shape[1] // 2
    body = functools.partial(_proj_body, norm_b=norm_b)
    cst = lambda i: (l, 0, 0)
    return pl.pallas_call(
        body,
        out_shape=jax.ShapeDtypeStruct((t, d), F32),
        grid=(t // tm,),
        in_specs=[
            pl.BlockSpec((tm, half), lambda i: (i, a_col)),
            pl.BlockSpec((tm, half), lambda i: (i, b_col)),
            pl.BlockSpec((None, 1, half), lambda i: (l, 0, gb_col)),
            _const_spec((None, 2 * half, d), cst),
            pl.BlockSpec((None, 1, d), cst),
            pl.BlockSpec((tm, d), lambda i: (i, 0)),
        ],
        out_specs=pl.BlockSpec((tm, d), lambda i: (i, 0)),
        compiler_params=_cparams("parallel"),
        name="proj",
    )(a, b, gb, w, gpost, x)


def _mem_kv_body(m_ref, g_ref, wk_ref, wv_ref, k_ref, v_ref):
    m = (_rms(m_ref[...]) * g_ref[...]).astype(BF)
    k_ref[...] = _dot(m, wk_ref[...]).astype(BF)
    v_ref[...] = _dot(m, wv_ref[...]).astype(BF)


def _mem_kv(mem, g, wk, wv, *, tn):
    rows, d = mem.shape
    depth = wk.shape[0]
    return pl.pallas_call(
        _mem_kv_body,
        out_shape=(jax.ShapeDtypeStruct((depth, rows, d), BF),
                   jax.ShapeDtypeStruct((depth, rows, d), BF)),
        grid=(depth, d // tn),
        in_specs=[
            pl.BlockSpec((rows, d), lambda l, n: (0, 0)),
            pl.BlockSpec((None, 1, d), lambda l, n: (l, 0, 0)),
            pl.BlockSpec((None, d, tn), lambda l, n: (l, 0, n)),
            pl.BlockSpec((None, d, tn), lambda l, n: (l, 0, n)),
        ],
        out_specs=(pl.BlockSpec((None, rows, tn), lambda l, n: (l, 0, n)),
                   pl.BlockSpec((None, rows, tn), lambda l, n: (l, 0, n))),
        compiler_params=_cparams("parallel", "parallel"),
        name="mem_kv",
    )(mem, g, wk, wv)


def _xattn_body(x_ref, g_ref, wq_ref, k_ref, v_ref, o_ref, *, scale):
    h = (_rms(x_ref[...]) * g_ref[...]).astype(BF)
    hd = x_ref.shape[-1] // X_HEADS
    for hh in range(X_HEADS):
        cs = slice(hh * hd, (hh + 1) * hd)
        qh = _dot(h, wq_ref[:, cs]).astype(BF)
        s = _dot_t(qh, k_ref[:, cs]) * scale
        e = jnp.exp(s - jnp.max(s, axis=-1, keepdims=True))
        p = e * (1.0 / jnp.sum(e, axis=-1, keepdims=True))
        o_ref[:, cs] = _dot(p.astype(BF), v_ref[:, cs]).astype(BF)


def _xattn(x, l, g, wq, kmem, vmem, *, seq, n_mem, tm):
    t, d = x.shape
    spt = seq // tm
    body = functools.partial(_xattn_body, scale=float((d // X_HEADS) ** -0.5))
    cst = lambda i: (l, 0, 0)
    return pl.pallas_call(
        body,
        out_shape=jax.ShapeDtypeStruct((t, d), BF),
        grid=(t // tm,),
        in_specs=[
            pl.BlockSpec((tm, d), lambda i: (i, 0)),
            pl.BlockSpec((None, 1, d), cst),
            _const_spec((None, d, d), cst),
            pl.BlockSpec((None, n_mem, d), lambda i: (l, i // spt, 0)),
            pl.BlockSpec((None, n_mem, d), lambda i: (l, i // spt, 0)),
        ],
        out_specs=pl.BlockSpec((tm, d), lambda i: (i, 0)),
        compiler_params=_cparams("parallel"),
        name="xattn",
    )(x, g, wq, kmem, vmem)


def _rope_tables(seq):
    inv = 1.0 / (ROPE_THETA ** (jnp.arange(0, ROPE, 2, dtype=F32) / ROPE))
    ang = jnp.arange(seq, dtype=F32)[:, None] * inv[None, :]
    ang = jnp.concatenate([ang, ang], axis=-1)
    pad = jnp.zeros((seq, 128 - ROPE), F32)
    return (jnp.concatenate([jnp.cos(ang), pad], axis=-1),
            jnp.concatenate([jnp.sin(ang), pad], axis=-1))


def _rot_cols(w):
    half = w.shape[-1] // 2
    return jnp.concatenate([-w[..., half:], w[..., :half]], axis=-1)


def _tile(n, want):
    t = min(n, want)
    while n % t:
        t //= 2
    return t


def _trunk(x3, mem3, p, depth, heads, cc):
    bsz, seq, d = x3.shape
    n_mem = mem3.shape[1]
    x = x3.reshape(bsz * seq, d)
    mem = mem3.reshape(bsz * n_mem, d)
    cosp, sinp = _rope_tables(seq)
    kmem, vmem = _mem_kv(mem, p['g_mem'], p['w_xk'], p['w_xv'], tn=_tile(d, 512))
    tm = _tile(seq, 512)
    tm_mix = _tile(seq, 256)
    tm_conv = _tile(seq, 256)
    dff = p['w_ffn1_gate'].shape[-1]
    tf = _tile(dff, 512)
    for l in range(depth):
        x = _ffn(x, l, p['g_ffn1_pre'], p['w_ffn1_gate'], p['w_ffn1_up'], p['w_ffn1_down'],
                 p['g_ffn1_post'], tm=tm, tf=tf)
        glu, q, k, v = _mix_in(x, l, p['g_mix_pre'], p['w_in_ext'], p['g_q_lat'], p['g_kv_lat'],
                               p['w_qm'], p['w_qr'], p['w_k'], p['w_v'], cosp, sinp,
                               seq=seq, tm=tm_mix, cc=cc, heads=heads)
        conv_n = _conv(glu, l, p['conv_dw_w'], p['conv_dw_b'], p['conv_ln_g'], p['conv_ln_b'],
                       p['g_group_out'], seq=seq, tm=tm_conv, rc=_tile(tm_conv, 64))
        attn = _attn(q, k, v, seq=seq, heads=heads, tq=_tile(seq, 512), tk=_tile(seq, 512))
        x = _proj(conv_n, attn, 0, 0, l, p['g_group_out'], 1, p['w_out'], p['g_mix_post'], x,
                  tm=tm, norm_b=True)
        o = _xattn(x, l, p['g_x_pre'], p['w_xq'], kmem, vmem, seq=seq, n_mem=n_mem, tm=tm)
        x = _proj(o, o, 0, 1, l, p['g_group_out'], 0, p['w_xo'], p['g_x_post'], x,
                  tm=tm, norm_b=False)
        x = _ffn(x, l, p['g_ffn2_pre'], p['w_ffn2_gate'], p['w_ffn2_up'], p['w_ffn2_down'],
                 p['g_ffn2_post'], tm=tm, tf=tf)
    return x.reshape(bsz, seq, d)


def kernel(x_prompt, x_sample, mem_prompt, mem_sample, g_ffn1_pre, g_ffn1_post, w_ffn1_gate, w_ffn1_up, w_ffn1_down, g_mix_pre, g_mix_post, w_in, conv_dw_w, conv_dw_b, conv_ln_g, conv_ln_b, g_q_lat, g_kv_lat, w_q_up, w_kv_up, g_group_out, w_out, g_x_pre, g_x_post, g_mem, w_xq, w_xk, w_xv, w_xo, g_ffn2_pre, g_ffn2_post, w_ffn2_gate, w_ffn2_up, w_ffn2_down):
    depth, d, _ = w_in.shape
    cc = conv_dw_w.shape[-1]
    ql, kvl = g_q_lat.shape[-1], g_kv_lat.shape[-1]
    heads = w_kv_up.shape[-1] // (NOPE + VDIM)
    o3 = 2 * cc + ql + kvl

    row = lambda g: g[:, None, :]
    p = dict(
        g_ffn1_pre=row(g_ffn1_pre), g_ffn1_post=row(g_ffn1_post),
        g_ffn2_pre=row(g_ffn2_pre), g_ffn2_post=row(g_ffn2_post),
        g_mix_pre=row(g_mix_pre), g_mix_post=row(g_mix_post),
        g_x_pre=row(g_x_pre), g_x_post=row(g_x_post), g_mem=row(g_mem),
        g_q_lat=row(g_q_lat), g_kv_lat=row(g_kv_lat), g_group_out=row(g_group_out),
        conv_dw_w=conv_dw_w, conv_dw_b=row(conv_dw_b), conv_ln_g=row(conv_ln_g),
        conv_ln_b=row(conv_ln_b),
        w_ffn1_gate=w_ffn1_gate.astype(BF), w_ffn1_up=w_ffn1_up.astype(BF),
        w_ffn1_down=w_ffn1_down.astype(BF),
        w_ffn2_gate=w_ffn2_gate.astype(BF), w_ffn2_up=w_ffn2_up.astype(BF),
        w_ffn2_down=w_ffn2_down.astype(BF),
        w_out=w_out.astype(BF), w_xq=w_xq.astype(BF), w_xk=w_xk.astype(BF),
        w_xv=w_xv.astype(BF), w_xo=w_xo.astype(BF),
    )
    w_kr = w_in[:, :, o3:]
    z64 = jnp.zeros((depth, d, 128 - ROPE), w_in.dtype)
    p['w_in_ext'] = jnp.concatenate([w_in[:, :, :o3], w_kr, z64, _rot_cols(w_kr), z64],
                                    axis=-1).astype(BF)
    wq4 = w_q_up.reshape(depth, ql, heads, NOPE + ROPE)
    wq_nope, wq_rope = wq4[..., :NOPE], wq4[..., NOPE:]
    zq = jnp.zeros((depth, ql, heads, 128 - ROPE), w_q_up.dtype)
    p['w_qm'] = jnp.concatenate([wq_rope, zq, wq_nope], axis=-1).reshape(
        depth, ql, heads * HEAD_W).astype(BF)
    p['w_qr'] = jnp.concatenate([_rot_cols(wq_rope), zq], axis=-1).reshape(
        depth, ql, heads * 128).astype(BF)
    wkv4 = w_kv_up.reshape(depth, kvl, heads, NOPE + VDIM)
    p['w_k'] = wkv4[..., :NOPE].reshape(depth, kvl, heads * NOPE).astype(BF)
    p['w_v'] = wkv4[..., NOPE:].reshape(depth, kvl, heads * VDIM).astype(BF)

    y_prompt = _trunk(x_prompt, mem_prompt, p, depth, heads, cc)
    y_sample = _trunk(x_sample, mem_sample, p, depth, heads, cc)
    return (y_prompt, y_sample)
```

```python
import functools

import jax
import jax.numpy as jnp
from jax import lax
from jax.experimental import pallas as pl
from jax.experimental.pallas import tpu as pltpu

EPS = 1e-6
LOG2_E = 1.4426950408889634
ROPE_THETA = 10000.0
NOPE = 128
ROPE = 64
VDIM = 128
HEAD_W = 2 * NOPE
X_HEADS = 4
HALO = 16
SUBLANES = 8
AHEAD = 1
LAG = 2
MAX_EXCESS = 60.0
ONES_ROWS = 16

BF = jnp.bfloat16
F32 = jnp.float32

VMEM_LIMIT = 56 * 1024 * 1024


def _cparams(*sem):
    return pltpu.CompilerParams(dimension_semantics=sem, vmem_limit_bytes=VMEM_LIMIT)


def _rms(x):
    return x * lax.rsqrt(jnp.mean(x * x, axis=-1, keepdims=True) + EPS)


def _dot(a, b):
    return jnp.dot(a, b, preferred_element_type=F32)


def _dot_t(a, b):
    return lax.dot_general(a, b, (((1,), (1,)), ((), ())), preferred_element_type=F32)


def _const_spec(shape, index_map):
    return pl.BlockSpec(shape, index_map, pipeline_mode=pl.Buffered(1))


def _ffn_body(x_ref, gpre_ref, wg_ref, wu_ref, wd_ref, gpost_ref, o_ref, h_ref, acc_ref):
    f = pl.program_id(1)

    @pl.when(f == 0)
    def _():
        h_ref[...] = (_rms(x_ref[...]) * gpre_ref[...]).astype(BF)
        acc_ref[...] = jnp.zeros_like(acc_ref)

    h = h_ref[...]
    a = _dot(h, wg_ref[...])
    b = _dot(h, wu_ref[...])
    z = (a * jax.nn.sigmoid(a) * b).astype(BF)
    acc_ref[...] += _dot(z, wd_ref[...])

    @pl.when(f == pl.num_programs(1) - 1)
    def _():
        o_ref[...] = x_ref[...] + 0.5 * (_rms(acc_ref[...]) * gpost_ref[...])


def _ffn(x, l, gpre, wg, wu, wd, gpost, *, tm, tf):
    t, d = x.shape
    dff = wg.shape[-1]
    return pl.pallas_call(
        _ffn_body,
        out_shape=jax.ShapeDtypeStruct((t, d), F32),
        grid=(t // tm, dff // tf),
        in_specs=[
            pl.BlockSpec((tm, d), lambda i, f: (i, 0)),
            pl.BlockSpec((None, 1, d), lambda i, f: (l, 0, 0)),
            pl.BlockSpec((None, d, tf), lambda i, f: (l, 0, f)),
            pl.BlockSpec((None, d, tf), lambda i, f: (l, 0, f)),
            pl.BlockSpec((None, tf, d), lambda i, f: (l, f, 0)),
            pl.BlockSpec((None, 1, d), lambda i, f: (l, 0, 0)),
        ],
        out_specs=pl.BlockSpec((tm, d), lambda i, f: (i, 0)),
        scratch_shapes=[pltpu.VMEM((tm, d), BF), pltpu.VMEM((tm, d), F32)],
        compiler_params=_cparams("parallel", "arbitrary"),
        name="ffn",
    )(x, gpre, wg, wu, wd, gpost)


def _mix_in_body(x_ref, gpre_ref, win_ref, gq_ref, gkv_ref, wqm_ref, wqr_ref, wk_ref, wv_ref,
                 cos_ref, sin_ref, glu_ref, qt_ref, k_ref, vt_ref, *, cc, ql, kvl, heads, scale):
    h = (_rms(x_ref[...]) * gpre_ref[...]).astype(BF)
    gc = min(cc, 512)
    for c in range(0, cc, gc):
        a = _dot(h, win_ref[:, c:c + gc])
        b = _dot(h, win_ref[:, cc + c:cc + c + gc])
        glu_ref[:, c:c + gc] = a * jax.nn.sigmoid(b)
    o1 = 2 * cc
    lat = _dot(h, win_ref[:, o1:])
    qn = (_rms(lat[:, :ql]) * gq_ref[...]).astype(BF)
    kvn = (_rms(lat[:, ql:ql + kvl]) * gkv_ref[...]).astype(BF)
    o3 = ql + kvl
    cosp = cos_ref[...]
    sinp = sin_ref[...]
    krope = (lat[:, o3:o3 + 128] * cosp + lat[:, o3 + 128:o3 + 256] * sinp).astype(BF)
    qm = _dot(qn, wqm_ref[...])
    qr = _dot(qn, wqr_ref[...])
    kn = _dot(kvn, wk_ref[...])
    vt_ref[...] = _dot(kvn, wv_ref[...]).T.astype(BF)
    for hh in range(heads):
        b0 = hh * HEAD_W
        qrope = qm[:, b0:b0 + 128] * cosp + qr[:, hh * 128:(hh + 1) * 128] * sinp
        qt_ref[b0:b0 + 128, :] = (qrope * scale).T.astype(BF)
        qt_ref[b0 + 128:b0 + HEAD_W, :] = (qm[:, b0 + 128:b0 + HEAD_W] * scale).T.astype(BF)
        k_ref[:, b0:b0 + 128] = krope
        k_ref[:, b0 + 128:b0 + HEAD_W] = kn[:, hh * NOPE:(hh + 1) * NOPE].astype(BF)


def _mix_in(x, l, gpre, win, gq, gkv, wqm, wqr, wk, wv, cosp, sinp, *, seq, tm, cc, heads):
    t, d = x.shape
    ql, kvl = gq.shape[-1], gkv.shape[-1]
    nwin = win.shape[-1]
    spt = seq // tm
    body = functools.partial(_mix_in_body, cc=cc, ql=ql, kvl=kvl, heads=heads,
                             scale=float((NOPE + ROPE) ** -0.5 * LOG2_E))
    cst = lambda i: (l, 0, 0)
    return pl.pallas_call(
        body,
        out_shape=(jax.ShapeDtypeStruct((t, cc), F32),
                   jax.ShapeDtypeStruct((heads * HEAD_W, t), BF),
                   jax.ShapeDtypeStruct((t, heads * HEAD_W), BF),
                   jax.ShapeDtypeStruct((heads * VDIM, t), BF)),
        grid=(t // tm,),
        in_specs=[
            pl.BlockSpec((tm, d), lambda i: (i, 0)),
            _const_spec((None, 1, d), cst),
            _const_spec((None, d, nwin), cst),
            _const_spec((None, 1, ql), cst),
            _const_spec((None, 1, kvl), cst),
            _const_spec((None, ql, heads * HEAD_W), cst),
            _const_spec((None, ql, heads * 128), cst),
            _const_spec((None, kvl, heads * NOPE), cst),
            _const_spec((None, kvl, heads * VDIM), cst),
            pl.BlockSpec((tm, 128), lambda i: (i % spt, 0)),
            pl.BlockSpec((tm, 128), lambda i: (i % spt, 0)),
        ],
        out_specs=(pl.BlockSpec((tm, cc), lambda i: (i, 0)),
                   pl.BlockSpec((heads * HEAD_W, tm), lambda i: (0, i)),
                   pl.BlockSpec((tm, heads * HEAD_W), lambda i: (i, 0)),
                   pl.BlockSpec((heads * VDIM, tm), lambda i: (0, i))),
        compiler_params=_cparams("parallel"),
        name="mix_in",
    )(x, gpre, win, gq, gkv, wqm, wqr, wk, wv, cosp, sinp)


def _conv_body(x_ref, prev_ref, next_ref, w_ref, b_ref, lng_ref, lnb_ref, gg_ref, o_ref,
               win_ref, sh_ref, conv_ref, *, tm, kw, spt, rc):
    i = pl.program_id(0)
    first = (i % spt) == 0
    last = (i % spt) == spt - 1
    win_ref[0:HALO, :] = jnp.where(first, 0.0, prev_ref[...])
    win_ref[HALO:HALO + tm, :] = x_ref[...]
    win_ref[HALO + tm:, :] = jnp.where(last, 0.0, next_ref[...])
    cc = x_ref.shape[-1]
    shift = HALO - kw // 2
    nsh = sh_ref.shape[1]
    for b in range(1, SUBLANES):
        sh_ref[b - 1] = win_ref[b:b + nsh, :]

    def tap(k, r, c):
        off = shift + k
        a, b = off // SUBLANES, off % SUBLANES
        r0 = r + a * SUBLANES
        if b == 0:
            return win_ref[r0:r0 + rc, c:c + 128]
        return sh_ref[b - 1, r0:r0 + rc, c:c + 128]

    for c in range(0, cc, 128):
        wk = [w_ref[k:k + 1, c:c + 128] for k in range(kw)]
        bias = b_ref[:, c:c + 128]
        for r in range(0, tm, rc):
            acc = wk[0] * tap(0, r, c)
            for k in range(1, kw):
                acc = acc + wk[k] * tap(k, r, c)
            conv_ref[r:r + rc, c:c + 128] = acc + bias
    hc = conv_ref[...]
    xc = hc - jnp.mean(hc, axis=-1, keepdims=True)
    y = xc * lax.rsqrt(jnp.mean(xc * xc, axis=-1, keepdims=True) + EPS) * lng_ref[...] + lnb_ref[...]
    s = y * jax.nn.sigmoid(y)
    o_ref[...] = (_rms(s) * gg_ref[...]).astype(BF)


def _conv(glu, l, w, b, lng, lnb, gg, *, seq, tm, rc):
    t, cc = glu.shape
    kw = w.shape[1]
    spt = seq // tm
    hb = tm // HALO
    nhb = t // HALO
    nsh = tm + SUBLANES * ((HALO - kw // 2 + kw - 1) // SUBLANES)
    assert kw // 2 <= HALO and nsh + SUBLANES - 1 <= tm + 2 * HALO
    body = functools.partial(_conv_body, tm=tm, kw=kw, spt=spt, rc=rc)
    cst = lambda i: (l, 0, 0)
    return pl.pallas_call(
        body,
        out_shape=jax.ShapeDtypeStruct((t, cc), BF),
        grid=(t // tm,),
        in_specs=[
            pl.BlockSpec((tm, cc), lambda i: (i, 0)),
            pl.BlockSpec((HALO, cc), lambda i: (jnp.maximum(i * hb - 1, 0), 0)),
            pl.BlockSpec((HALO, cc), lambda i: (jnp.minimum((i + 1) * hb, nhb - 1), 0)),
            pl.BlockSpec((None, kw, cc), cst),
            pl.BlockSpec((None, 1, cc), cst),
            pl.BlockSpec((None, 1, cc), cst),
            pl.BlockSpec((None, 1, cc), cst),
            pl.BlockSpec((None, 1, cc), lambda i: (l, 0, 0)),
        ],
        out_specs=pl.BlockSpec((tm, cc), lambda i: (i, 0)),
        scratch_shapes=[pltpu.VMEM((tm + 2 * HALO, cc), F32),
                        pltpu.VMEM((SUBLANES - 1, nsh, cc), F32),
                        pltpu.VMEM((tm, cc), F32)],
        compiler_params=_cparams("parallel"),
        name="conv",
    )(glu, glu, glu, w, b, lng, lnb, gg)


def _attn_stream(qt, k_ref, vt_ref, *, tk, nk, lag):
    tq = qt.shape[1]
    ones = jnp.ones((ONES_ROWS, tk), BF)
    score = lambda c: _dot(k_ref[c * tk:(c + 1) * tk, :], qt)
    scores = [score(c) for c in range(min(AHEAD, nk))]
    run_max = []
    excess = jnp.zeros((1, tq), F32)
    acc = m_acc = None
    for c in range(nk):
        if c + AHEAD < nk:
            scores.append(score(c + AHEAD))
        st = scores.pop(0)
        cmax = jnp.max(st, axis=0, keepdims=True)
        run_max.append(cmax if c == 0 else jnp.maximum(run_max[-1], cmax))
        if c < lag:
            m_use = run_max[c]
        else:
            m_use = run_max[max(c - lag, lag - 1)]
            excess = jnp.maximum(excess, cmax - m_use)
        p = jnp.exp2(st - m_use).astype(BF)
        vt_ext = jnp.concatenate([vt_ref[:, c * tk:(c + 1) * tk], ones], axis=0)
        pv = _dot(vt_ext, p)
        acc = pv if c == 0 else jnp.exp2(m_acc - m_use) * acc + pv
        m_acc = m_use
    return acc, excess


def _attn_body(qt_ref, k_ref, vt_ref, o_ref, *, tk, nk):
    qt = qt_ref[...]

    def finish(acc):
        o_ref[...] = (acc[:VDIM] / acc[VDIM:VDIM + 1]).T

    acc, excess = _attn_stream(qt, k_ref, vt_ref, tk=tk, nk=nk, lag=min(LAG, nk))
    finish(acc)

    @pl.when(jnp.max(excess) > MAX_EXCESS)
    def _():
        finish(_attn_stream(qt, k_ref, vt_ref, tk=tk, nk=nk, lag=nk)[0])


def _attn(qt, k, vt, *, seq, heads, tq, tk):
    t = k.shape[0]
    bsz = t // seq
    qpt = seq // tq
    body = functools.partial(_attn_body, tk=tk, nk=seq // tk)
    return pl.pallas_call(
        body,
        out_shape=jax.ShapeDtypeStruct((t, heads * VDIM), F32),
        grid=(bsz, heads, qpt),
        in_specs=[
            pl.BlockSpec((HEAD_W, tq), lambda b, h, i: (h, b * qpt + i)),
            pl.BlockSpec((seq, HEAD_W), lambda b, h, i: (b, h)),
            pl.BlockSpec((VDIM, seq), lambda b, h, i: (h, b)),
        ],
        out_specs=pl.BlockSpec((tq, VDIM), lambda b, h, i: (b * qpt + i, h)),
        compiler_params=_cparams("parallel", "parallel", "arbitrary"),
        name="mla_attn",
    )(qt, k, vt)


def _proj_body(a_ref, b_ref, gb_ref, w_ref, gpost_ref, x_ref, o_ref, *, norm_b):
    half = a_ref.shape[-1]
    b = b_ref[...]
    if norm_b:
        b = (_rms(b) * gb_ref[...]).astype(BF)
    t = _dot(a_ref[...], w_ref[:half, :]) + _dot(b, w_ref[half:, :])
    o_ref[...] = x_ref[...] + _rms(t) * gpost_ref[...]


def _proj(a, b, a_col, b_col, l, gb, gb_col, w, gpost, x, *, tm, norm_b):
    t, d = x.shape
    half = w.shape[1] // 2
    body = functools.partial(_proj_body, norm_b=norm_b)
    cst = lambda i: (l, 0, 0)
    return pl.pallas_call(
        body,
        out_shape=jax.ShapeDtypeStruct((t, d), F32),
        grid=(t // tm,),
        in_specs=[
            pl.BlockSpec((tm, half), lambda i: (i, a_col)),
            pl.BlockSpec((tm, half), lambda i: (i, b_col)),
            pl.BlockSpec((None, 1, half), lambda i: (l, 0, gb_col)),
            _const_spec((None, 2 * half, d), cst),
            pl.BlockSpec((None, 1, d), cst),
            pl.BlockSpec((tm, d), lambda i: (i, 0)),
        ],
        out_specs=pl.BlockSpec((tm, d), lambda i: (i, 0)),
        compiler_params=_cparams("parallel"),
        name="proj",
    )(a, b, gb, w, gpost, x)


def _mem_kv_body(m_ref, g_ref, wk_ref, wv_ref, k_ref, v_ref):
    m = (_rms(m_ref[...]) * g_ref[...]).astype(BF)
    k_ref[...] = _dot(m, wk_ref[...]).astype(BF)
    v_ref[...] = _dot(m, wv_ref[...]).astype(BF)


def _mem_kv(mem, g, wk, wv, *, tn):
    rows, d = mem.shape
    depth = wk.shape[0]
    return pl.pallas_call(
        _mem_kv_body,
        out_shape=(jax.ShapeDtypeStruct((depth, rows, d), BF),
                   jax.ShapeDtypeStruct((depth, rows, d), BF)),
        grid=(depth, d // tn),
        in_specs=[
            pl.BlockSpec((rows, d), lambda l, n: (0, 0)),
            pl.BlockSpec((None, 1, d), lambda l, n: (l, 0, 0)),
            pl.BlockSpec((None, d, tn), lambda l, n: (l, 0, n)),
            pl.BlockSpec((None, d, tn), lambda l, n: (l, 0, n)),
        ],
        out_specs=(pl.BlockSpec((None, rows, tn), lambda l, n: (l, 0, n)),
                   pl.BlockSpec((None, rows, tn), lambda l, n: (l, 0, n))),
        compiler_params=_cparams("parallel", "parallel"),
        name="mem_kv",
    )(mem, g, wk, wv)


def _xattn_body(x_ref, g_ref, wq_ref, k_ref, v_ref, o_ref, *, scale):
    h = (_rms(x_ref[...]) * g_ref[...]).astype(BF)
    hd = x_ref.shape[-1] // X_HEADS
    cols = [slice(hh * hd, (hh + 1) * hd) for hh in range(X_HEADS)]
    qproj = lambda hh: _dot(h, wq_ref[:, cols[hh]]).astype(BF)
    q_next = qproj(0)
    for hh in range(X_HEADS):
        cs = cols[hh]
        s = _dot_t(q_next, k_ref[:, cs]) * scale
        if hh + 1 < X_HEADS:
            q_next = qproj(hh + 1)
        e = jnp.exp(s - jnp.max(s, axis=-1, keepdims=True))
        p = e * (1.0 / jnp.sum(e, axis=-1, keepdims=True))
        o_ref[:, cs] = _dot(p.astype(BF), v_ref[:, cs]).astype(BF)


def _xattn(x, l, g, wq, kmem, vmem, *, seq, n_mem, tm):
    t, d = x.shape
    spt = seq // tm
    body = functools.partial(_xattn_body, scale=float((d // X_HEADS) ** -0.5))
    cst = lambda i: (l, 0, 0)
    return pl.pallas_call(
        body,
        out_shape=jax.ShapeDtypeStruct((t, d), BF),
        grid=(t // tm,),
        in_specs=[
            pl.BlockSpec((tm, d), lambda i: (i, 0)),
            pl.BlockSpec((None, 1, d), cst),
            _const_spec((None, d, d), cst),
            pl.BlockSpec((None, n_mem, d), lambda i: (l, i // spt, 0)),
            pl.BlockSpec((None, n_mem, d), lambda i: (l, i // spt, 0)),
        ],
        out_specs=pl.BlockSpec((tm, d), lambda i: (i, 0)),
        compiler_params=_cparams("parallel"),
        name="xattn",
    )(x, g, wq, kmem, vmem)


def _rope_tables(seq):
    inv = 1.0 / (ROPE_THETA ** (jnp.arange(0, ROPE, 2, dtype=F32) / ROPE))
    ang = jnp.arange(seq, dtype=F32)[:, None] * inv[None, :]
    ang = jnp.concatenate([ang, ang], axis=-1)
    pad = jnp.zeros((seq, 128 - ROPE), F32)
    return (jnp.concatenate([jnp.cos(ang), pad], axis=-1),
            jnp.concatenate([jnp.sin(ang), pad], axis=-1))


def _rot_cols(w):
    half = w.shape[-1] // 2
    return jnp.concatenate([-w[..., half:], w[..., :half]], axis=-1)


def _tile(n, want):
    t = min(n, want)
    while n % t:
        t //= 2
    return t


def _trunk(x3, mem3, p, depth, heads, cc):
    bsz, seq, d = x3.shape
    n_mem = mem3.shape[1]
    x = x3.reshape(bsz * seq, d)
    mem = mem3.reshape(bsz * n_mem, d)
    cosp, sinp = _rope_tables(seq)
    kmem, vmem = _mem_kv(mem, p['g_mem'], p['w_xk'], p['w_xv'], tn=_tile(d, 512))
    tm = _tile(seq, 512)
    tm_mix = _tile(seq, 256)
    tm_conv = _tile(seq, 256)
    dff = p['w_ffn1_gate'].shape[-1]
    tf = _tile(dff, 512)
    for l in range(depth):
        x = _ffn(x, l, p['g_ffn1_pre'], p['w_ffn1_gate'], p['w_ffn1_up'], p['w_ffn1_down'],
                 p['g_ffn1_post'], tm=tm, tf=tf)
        glu, q, k, v = _mix_in(x, l, p['g_mix_pre'], p['w_in_ext'], p['g_q_lat'], p['g_kv_lat'],
                               p['w_qm'], p['w_qr'], p['w_k'], p['w_v'], cosp, sinp,
                               seq=seq, tm=tm_mix, cc=cc, heads=heads)
        conv_n = _conv(glu, l, p['conv_dw_w'], p['conv_dw_b'], p['conv_ln_g'], p['conv_ln_b'],
                       p['g_group_out'], seq=seq, tm=tm_conv, rc=_tile(tm_conv, 64))
        attn = _attn(q, k, v, seq=seq, heads=heads, tq=_tile(seq, 512), tk=_tile(seq, 512))
        x = _proj(conv_n, attn, 0, 0, l, p['g_group_out'], 1, p['w_out'], p['g_mix_post'], x,
                  tm=tm, norm_b=True)
        o = _xattn(x, l, p['g_x_pre'], p['w_xq'], kmem, vmem, seq=seq, n_mem=n_mem, tm=tm)
        x = _proj(o, o, 0, 1, l, p['g_group_out'], 0, p['w_xo'], p['g_x_post'], x,
                  tm=tm, norm_b=False)
        x = _ffn(x, l, p['g_ffn2_pre'], p['w_ffn2_gate'], p['w_ffn2_up'], p['w_ffn2_down'],
                 p['g_ffn2_post'], tm=tm, tf=tf)
    return x.reshape(bsz, seq, d)


def kernel(x_prompt, x_sample, mem_prompt, mem_sample, g_ffn1_pre, g_ffn1_post, w_ffn1_gate, w_ffn1_up, w_ffn1_down, g_mix_pre, g_mix_post, w_in, conv_dw_w, conv_dw_b, conv_ln_g, conv_ln_b, g_q_lat, g_kv_lat, w_q_up, w_kv_up, g_group_out, w_out, g_x_pre, g_x_post, g_mem, w_xq, w_xk, w_xv, w_xo, g_ffn2_pre, g_ffn2_post, w_ffn2_gate, w_ffn2_up, w_ffn2_down):
    depth, d, _ = w_in.shape
    cc = conv_dw_w.shape[-1]
    ql, kvl = g_q_lat.shape[-1], g_kv_lat.shape[-1]
    heads = w_kv_up.shape[-1] // (NOPE + VDIM)
    o3 = 2 * cc + ql + kvl

    row = lambda g: g[:, None, :]
    p = dict(
        g_ffn1_pre=row(g_ffn1_pre), g_ffn1_post=row(g_ffn1_post),
        g_ffn2_pre=row(g_ffn2_pre), g_ffn2_post=row(g_ffn2_post),
        g_mix_pre=row(g_mix_pre), g_mix_post=row(g_mix_post),
        g_x_pre=row(g_x_pre), g_x_post=row(g_x_post), g_mem=row(g_mem),
        g_q_lat=row(g_q_lat), g_kv_lat=row(g_kv_lat), g_group_out=row(g_group_out),
        conv_dw_w=conv_dw_w, conv_dw_b=row(conv_dw_b), conv_ln_g=row(conv_ln_g),
        conv_ln_b=row(conv_ln_b),
        w_ffn1_gate=w_ffn1_gate.astype(BF), w_ffn1_up=w_ffn1_up.astype(BF),
        w_ffn1_down=w_ffn1_down.astype(BF),
        w_ffn2_gate=w_ffn2_gate.astype(BF), w_ffn2_up=w_ffn2_up.astype(BF),
        w_ffn2_down=w_ffn2_down.astype(BF),
        w_out=w_out.astype(BF), w_xq=w_xq.astype(BF), w_xk=w_xk.astype(BF),
        w_xv=w_xv.astype(BF), w_xo=w_xo.astype(BF),
    )
    w_kr = w_in[:, :, o3:]
    z64 = jnp.zeros((depth, d, 128 - ROPE), w_in.dtype)
    p['w_in_ext'] = jnp.concatenate([w_in[:, :, :o3], w_kr, z64, _rot_cols(w_kr), z64],
                                    axis=-1).astype(BF)
    wq4 = w_q_up.reshape(depth, ql, heads, NOPE + ROPE)
    wq_nope, wq_rope = wq4[..., :NOPE], wq4[..., NOPE:]
    zq = jnp.zeros((depth, ql, heads, 128 - ROPE), w_q_up.dtype)
    p['w_qm'] = jnp.concatenate([wq_rope, zq, wq_nope], axis=-1).reshape(
        depth, ql, heads * HEAD_W).astype(BF)
    p['w_qr'] = jnp.concatenate([_rot_cols(wq_rope), zq], axis=-1).reshape(
        depth, ql, heads * 128).astype(BF)
    wkv4 = w_kv_up.reshape(depth, kvl, heads, NOPE + VDIM)
    p['w_k'] = wkv4[..., :NOPE].reshape(depth, kvl, heads * NOPE).astype(BF)
    p['w_v'] = wkv4[..., NOPE:].reshape(depth, kvl, heads * VDIM).astype(BF)

    y_prompt = _trunk(x_prompt, mem_prompt, p, depth, heads, cc)
    y_sample = _trunk(x_sample, mem_sample, p, depth, heads, cc)
    return (y_prompt, y_sample)
```

```python
import functools

import jax
import jax.numpy as jnp
from jax import lax
from jax.experimental import pallas as pl
from jax.experimental.pallas import tpu as pltpu

EPS = 1e-6
LOG2_E = 1.4426950408889634
ROPE_THETA = 10000.0
NOPE = 128
ROPE = 64
VDIM = 128
HEAD_W = 2 * NOPE
X_HEADS = 4
HALO = 16
SUBLANES = 8
AHEAD = 1
LAG = 2
MAX_EXCESS = 60.0
EDGE_BLOCKS = 4
ONES_ROWS = 16

BF = jnp.bfloat16
F32 = jnp.float32

VMEM_LIMIT = 60 * 1024 * 1024


def _cparams(*sem):
    return pltpu.CompilerParams(dimension_semantics=sem, vmem_limit_bytes=VMEM_LIMIT)


def _rms(x):
    return x * lax.rsqrt(jnp.mean(x * x, axis=-1, keepdims=True) + EPS)


def _dot(a, b):
    return jnp.dot(a, b, preferred_element_type=F32)


def _dot_t(a, b):
    return lax.dot_general(a, b, (((1,), (1,)), ((), ())), preferred_element_type=F32)


def _const_spec(shape, index_map):
    return pl.BlockSpec(shape, index_map, pipeline_mode=pl.Buffered(1))


def _row_blocks(n, parts):
    step = n // parts
    return [slice(j * step, (j + 1) * step) for j in range(parts)]


def _ffn_body(x_ref, gpre_ref, wg_ref, wu_ref, wd_ref, gpost_ref, o_ref, h_ref):
    f = pl.program_id(1)
    last = pl.num_programs(1) - 1
    tm = x_ref.shape[0]

    def step(r, first, final):
        if first:
            h = (_rms(x_ref[r, :]) * gpre_ref[...]).astype(BF)
            h_ref[r, :] = h
        else:
            h = h_ref[r, :]
        a = _dot(h, wg_ref[...])
        b = _dot(h, wu_ref[...])
        z = (a * jax.nn.sigmoid(a) * b).astype(BF)
        acc = _dot(z, wd_ref[...])
        if not first:
            acc = o_ref[r, :] + acc
        if final:
            acc = x_ref[r, :] + 0.5 * (_rms(acc) * gpost_ref[...])
        o_ref[r, :] = acc

    @pl.when(f == 0)
    def _():
        for r in _row_blocks(tm, EDGE_BLOCKS):
            step(r, True, False)

    @pl.when(jnp.logical_and(f > 0, f < last))
    def _():
        step(slice(None), False, False)

    @pl.when(f == last)
    def _():
        for r in _row_blocks(tm, EDGE_BLOCKS):
            step(r, False, True)


def _ffn(x, l, gpre, wg, wu, wd, gpost, *, tm, tf):
    t, d = x.shape
    dff = wg.shape[-1]
    assert dff // tf >= 2 and tm % (16 * EDGE_BLOCKS) == 0
    return pl.pallas_call(
        _ffn_body,
        out_shape=jax.ShapeDtypeStruct((t, d), F32),
        grid=(t // tm, dff // tf),
        in_specs=[
            pl.BlockSpec((tm, d), lambda i, f: (i, 0)),
            pl.BlockSpec((None, 1, d), lambda i, f: (l, 0, 0)),
            pl.BlockSpec((None, d, tf), lambda i, f: (l, 0, f)),
            pl.BlockSpec((None, d, tf), lambda i, f: (l, 0, f)),
            pl.BlockSpec((None, tf, d), lambda i, f: (l, f, 0)),
            pl.BlockSpec((None, 1, d), lambda i, f: (l, 0, 0)),
        ],
        out_specs=pl.BlockSpec((tm, d), lambda i, f: (i, 0)),
        scratch_shapes=[pltpu.VMEM((tm, d), BF)],
        compiler_params=_cparams("parallel", "arbitrary"),
        name="ffn",
    )(x, gpre, wg, wu, wd, gpost)


def _mix_in_body(x_ref, gpre_ref, win_ref, gq_ref, gkv_ref, wqm_ref, wqr_ref, wk_ref, wv_ref,
                 cos_ref, sin_ref, glu_ref, qt_ref, k_ref, vt_ref, *, cc, ql, kvl, heads, scale):
    h = (_rms(x_ref[...]) * gpre_ref[...]).astype(BF)
    gc = min(cc, 512)
    for c in range(0, cc, gc):
        a = _dot(h, win_ref[:, c:c + gc])
        b = _dot(h, win_ref[:, cc + c:cc + c + gc])
        glu_ref[:, c:c + gc] = a * jax.nn.sigmoid(b)
    o1 = 2 * cc
    lat = _dot(h, win_ref[:, o1:])
    qn = (_rms(lat[:, :ql]) * gq_ref[...]).astype(BF)
    kvn = (_rms(lat[:, ql:ql + kvl]) * gkv_ref[...]).astype(BF)
    o3 = ql + kvl
    cosp = cos_ref[...]
    sinp = sin_ref[...]
    krope = (lat[:, o3:o3 + 128] * cosp + lat[:, o3 + 128:o3 + 256] * sinp).astype(BF)
    qm = _dot(qn, wqm_ref[...])
    qr = _dot(qn, wqr_ref[...])
    kn = _dot(kvn, wk_ref[...])
    vt_ref[...] = _dot(kvn, wv_ref[...]).T.astype(BF)
    for hh in range(heads):
        b0 = hh * HEAD_W
        qrope = qm[:, b0:b0 + 128] * cosp + qr[:, hh * 128:(hh + 1) * 128] * sinp
        qt_ref[b0:b0 + 128, :] = (qrope * scale).T.astype(BF)
        qt_ref[b0 + 128:b0 + HEAD_W, :] = (qm[:, b0 + 128:b0 + HEAD_W] * scale).T.astype(BF)
        k_ref[:, b0:b0 + 128] = krope
        k_ref[:, b0 + 128:b0 + HEAD_W] = kn[:, hh * NOPE:(hh + 1) * NOPE].astype(BF)


def _mix_in(x, l, gpre, win, gq, gkv, wqm, wqr, wk, wv, cosp, sinp, *, seq, tm, cc, heads):
    t, d = x.shape
    ql, kvl = gq.shape[-1], gkv.shape[-1]
    nwin = win.shape[-1]
    spt = seq // tm
    body = functools.partial(_mix_in_body, cc=cc, ql=ql, kvl=kvl, heads=heads,
                             scale=float((NOPE + ROPE) ** -0.5 * LOG2_E))
    cst = lambda i: (l, 0, 0)
    return pl.pallas_call(
        body,
        out_shape=(jax.ShapeDtypeStruct((t, cc), F32),
                   jax.ShapeDtypeStruct((heads * HEAD_W, t), BF),
                   jax.ShapeDtypeStruct((t, heads * HEAD_W), BF),
                   jax.ShapeDtypeStruct((heads * VDIM, t), BF)),
        grid=(t // tm,),
        in_specs=[
            pl.BlockSpec((tm, d), lambda i: (i, 0)),
            _const_spec((None, 1, d), cst),
            _const_spec((None, d, nwin), cst),
            _const_spec((None, 1, ql), cst),
            _const_spec((None, 1, kvl), cst),
            _const_spec((None, ql, heads * HEAD_W), cst),
            _const_spec((None, ql, heads * 128), cst),
            _const_spec((None, kvl, heads * NOPE), cst),
            _const_spec((None, kvl, heads * VDIM), cst),
            pl.BlockSpec((tm, 128), lambda i: (i % spt, 0)),
            pl.BlockSpec((tm, 128), lambda i: (i % spt, 0)),
        ],
        out_specs=(pl.BlockSpec((tm, cc), lambda i: (i, 0)),
                   pl.BlockSpec((heads * HEAD_W, tm), lambda i: (0, i)),
                   pl.BlockSpec((tm, heads * HEAD_W), lambda i: (i, 0)),
                   pl.BlockSpec((heads * VDIM, tm), lambda i: (0, i))),
        compiler_params=_cparams("parallel"),
        name="mix_in",
    )(x, gpre, win, gq, gkv, wqm, wqr, wk, wv, cosp, sinp)


def _conv_body(x_ref, prev_ref, next_ref, w_ref, b_ref, lng_ref, lnb_ref, gg_ref, o_ref,
               win_ref, sh_ref, conv_ref, *, tm, kw, spt, rc):
    i = pl.program_id(0)
    first = (i % spt) == 0
    last = (i % spt) == spt - 1
    win_ref[0:HALO, :] = jnp.where(first, 0.0, prev_ref[...])
    win_ref[HALO:HALO + tm, :] = x_ref[...]
    win_ref[HALO + tm:, :] = jnp.where(last, 0.0, next_ref[...])
    cc = x_ref.shape[-1]
    shift = HALO - kw // 2
    nsh = sh_ref.shape[1]
    for b in range(1, SUBLANES):
        sh_ref[b - 1] = win_ref[b:b + nsh, :]

    def tap(k, r, c):
        off = shift + k
        a, b = off // SUBLANES, off % SUBLANES
        r0 = r + a * SUBLANES
        if b == 0:
            return win_ref[r0:r0 + rc, c:c + 128]
        return sh_ref[b - 1, r0:r0 + rc, c:c + 128]

    for c in range(0, cc, 128):
        wk = [w_ref[k:k + 1, c:c + 128] for k in range(kw)]
        bias = b_ref[:, c:c + 128]
        for r in range(0, tm, rc):
            acc = wk[0] * tap(0, r, c)
            for k in range(1, kw):
                acc = acc + wk[k] * tap(k, r, c)
            conv_ref[r:r + rc, c:c + 128] = acc + bias
    hc = conv_ref[...]
    xc = hc - jnp.mean(hc, axis=-1, keepdims=True)
    y = xc * lax.rsqrt(jnp.mean(xc * xc, axis=-1, keepdims=True) + EPS) * lng_ref[...] + lnb_ref[...]
    s = y * jax.nn.sigmoid(y)
    o_ref[...] = (_rms(s) * gg_ref[...]).astype(BF)


def _conv(glu, l, w, b, lng, lnb, gg, *, seq, tm, rc):
    t, cc = glu.shape
    kw = w.shape[1]
    spt = seq // tm
    hb = tm // HALO
    nhb = t // HALO
    nsh = tm + SUBLANES * ((HALO - kw // 2 + kw - 1) // SUBLANES)
    assert kw // 2 <= HALO and nsh + SUBLANES - 1 <= tm + 2 * HALO
    body = functools.partial(_conv_body, tm=tm, kw=kw, spt=spt, rc=rc)
    cst = lambda i: (l, 0, 0)
    return pl.pallas_call(
        body,
        out_shape=jax.ShapeDtypeStruct((t, cc), BF),
        grid=(t // tm,),
        in_specs=[
            pl.BlockSpec((tm, cc), lambda i: (i, 0)),
            pl.BlockSpec((HALO, cc), lambda i: (jnp.maximum(i * hb - 1, 0), 0)),
            pl.BlockSpec((HALO, cc), lambda i: (jnp.minimum((i + 1) * hb, nhb - 1), 0)),
            pl.BlockSpec((None, kw, cc), cst),
            pl.BlockSpec((None, 1, cc), cst),
            pl.BlockSpec((None, 1, cc), cst),
            pl.BlockSpec((None, 1, cc), cst),
            pl.BlockSpec((None, 1, cc), lambda i: (l, 0, 0)),
        ],
        out_specs=pl.BlockSpec((tm, cc), lambda i: (i, 0)),
        scratch_shapes=[pltpu.VMEM((tm + 2 * HALO, cc), F32),
                        pltpu.VMEM((SUBLANES - 1, nsh, cc), F32),
                        pltpu.VMEM((tm, cc), F32)],
        compiler_params=_cparams("parallel"),
        name="conv",
    )(glu, glu, glu, w, b, lng, lnb, gg)


def _attn_stream(qt, k_ref, vt_ref, *, tk, nk, lag):
    tq = qt.shape[1]
    ones = jnp.ones((ONES_ROWS, tk), BF)
    score = lambda c: _dot(k_ref[c * tk:(c + 1) * tk, :], qt)
    scores = [score(c) for c in range(min(AHEAD, nk))]
    run_max = []
    excess = jnp.zeros((1, tq), F32)
    acc = m_acc = None
    for c in range(nk):
        if c + AHEAD < nk:
            scores.append(score(c + AHEAD))
        st = scores.pop(0)
        cmax = jnp.max(st, axis=0, keepdims=True)
        run_max.append(cmax if c == 0 else jnp.maximum(run_max[-1], cmax))
        if c < lag:
            m_use = run_max[c]
        else:
            m_use = run_max[max(c - lag, lag - 1)]
            excess = jnp.maximum(excess, cmax - m_use)
        p = jnp.exp2(st - m_use).astype(BF)
        vt_ext = jnp.concatenate([vt_ref[:, c * tk:(c + 1) * tk], ones], axis=0)
        pv = _dot(vt_ext, p)
        acc = pv if c == 0 else jnp.exp2(m_acc - m_use) * acc + pv
        m_acc = m_use
    return acc, excess


def _attn_body(qt_ref, k_ref, vt_ref, o_ref, *, tk, nk):
    qt = qt_ref[...]

    def finish(acc):
        o_ref[...] = (acc[:VDIM] / acc[VDIM:VDIM + 1]).T

    acc, excess = _attn_stream(qt, k_ref, vt_ref, tk=tk, nk=nk, lag=min(LAG, nk))
    finish(acc)

    @pl.when(jnp.max(excess) > MAX_EXCESS)
    def _():
        finish(_attn_stream(qt, k_ref, vt_ref, tk=tk, nk=nk, lag=nk)[0])


def _attn(qt, k, vt, *, seq, heads, tq, tk):
    t = k.shape[0]
    bsz = t // seq
    qpt = seq // tq
    body = functools.partial(_attn_body, tk=tk, nk=seq // tk)
    return pl.pallas_call(
        body,
        out_shape=jax.ShapeDtypeStruct((t, heads * VDIM), F32),
        grid=(bsz, heads, qpt),
        in_specs=[
            pl.BlockSpec((HEAD_W, tq), lambda b, h, i: (h, b * qpt + i)),
            pl.BlockSpec((seq, HEAD_W), lambda b, h, i: (b, h)),
            pl.BlockSpec((VDIM, seq), lambda b, h, i: (h, b)),
        ],
        out_specs=pl.BlockSpec((tq, VDIM), lambda b, h, i: (b * qpt + i, h)),
        compiler_params=_cparams("parallel", "parallel", "arbitrary"),
        name="mla_attn",
    )(qt, k, vt)


def _proj_body(a_ref, b_ref, gb_ref, w_ref, gpost_ref, x_ref, o_ref, *, norm_b):
    half = a_ref.shape[-1]
    b = b_ref[...]
    if norm_b:
        b = (_rms(b) * gb_ref[...]).astype(BF)
    t = _dot(a_ref[...], w_ref[:half, :]) + _dot(b, w_ref[half:, :])
    o_ref[...] = x_ref[...] + _rms(t) * gpost_ref[...]


def _proj(a, b, a_col, b_col, l, gb, gb_col, w, gpost, x, *, tm, norm_b):
    t, d = x.shape
    half = w.shape[1] // 2
    body = functools.partial(_proj_body, norm_b=norm_b)
    cst = lambda i: (l, 0, 0)
    return pl.pallas_call(
        body,
        out_shape=jax.ShapeDtypeStruct((t, d), F32),
        grid=(t // tm,),
        in_specs=[
            pl.BlockSpec((tm, half), lambda i: (i, a_col)),
            pl.BlockSpec((tm, half), lambda i: (i, b_col)),
            pl.BlockSpec((None, 1, half), lambda i: (l, 0, gb_col)),
            _const_spec((None, 2 * half, d), cst),
            pl.BlockSpec((None, 1, d), cst),
            pl.BlockSpec((tm, d), lambda i: (i, 0)),
        ],
        out_specs=pl.BlockSpec((tm, d), lambda i: (i, 0)),
        compiler_params=_cparams("parallel"),
        name="proj",
    )(a, b, gb, w, gpost, x)


def _mem_kv_body(m_ref, g_ref, wk_ref, wv_ref, k_ref, v_ref):
    m = (_rms(m_ref[...]) * g_ref[...]).astype(BF)
    k_ref[...] = _dot(m, wk_ref[...]).astype(BF)
    v_ref[...] = _dot(m, wv_ref[...]).astype(BF)


def _mem_kv(mem, g, wk, wv, *, tn):
    rows, d = mem.shape
    depth = wk.shape[0]
    return pl.pallas_call(
        _mem_kv_body,
        out_shape=(jax.ShapeDtypeStruct((depth, rows, d), BF),
                   jax.ShapeDtypeStruct((depth, rows, d), BF)),
        grid=(depth, d // tn),
        in_specs=[
            pl.BlockSpec((rows, d), lambda l, n: (0, 0)),
            pl.BlockSpec((None, 1, d), lambda l, n: (l, 0, 0)),
            pl.BlockSpec((None, d, tn), lambda l, n: (l, 0, n)),
            pl.BlockSpec((None, d, tn), lambda l, n: (l, 0, n)),
        ],
        out_specs=(pl.BlockSpec((None, rows, tn), lambda l, n: (l, 0, n)),
                   pl.BlockSpec((None, rows, tn), lambda l, n: (l, 0, n))),
        compiler_params=_cparams("parallel", "parallel"),
        name="mem_kv",
    )(mem, g, wk, wv)


def _xattn_body(x_ref, g_ref, wq_ref, k_ref, v_ref, o_ref, *, scale):
    h = (_rms(x_ref[...]) * g_ref[...]).astype(BF)
    hd = x_ref.shape[-1] // X_HEADS
    cols = [slice(hh * hd, (hh + 1) * hd) for hh in range(X_HEADS)]
    qproj = lambda hh: _dot(h, wq_ref[:, cols[hh]]).astype(BF)
    q_next = qproj(0)
    for hh in range(X_HEADS):
        cs = cols[hh]
        s = _dot_t(q_next, k_ref[:, cs]) * scale
        if hh + 1 < X_HEADS:
            q_next = qproj(hh + 1)
        e = jnp.exp(s - jnp.max(s, axis=-1, keepdims=True))
        p = e * (1.0 / jnp.sum(e, axis=-1, keepdims=True))
        o_ref[:, cs] = _dot(p.astype(BF), v_ref[:, cs]).astype(BF)


def _xattn(x, l, g, wq, kmem, vmem, *, seq, n_mem, tm):
    t, d = x.shape
    spt = seq // tm
    body = functools.partial(_xattn_body, scale=float((d // X_HEADS) ** -0.5))
    cst = lambda i: (l, 0, 0)
    return pl.pallas_call(
        body,
        out_shape=jax.ShapeDtypeStruct((t, d), BF),
        grid=(t // tm,),
        in_specs=[
            pl.BlockSpec((tm, d), lambda i: (i, 0)),
            pl.BlockSpec((None, 1, d), cst),
            _const_spec((None, d, d), cst),
            pl.BlockSpec((None, n_mem, d), lambda i: (l, i // spt, 0)),
            pl.BlockSpec((None, n_mem, d), lambda i: (l, i // spt, 0)),
        ],
        out_specs=pl.BlockSpec((tm, d), lambda i: (i, 0)),
        compiler_params=_cparams("parallel"),
        name="xattn",
    )(x, g, wq, kmem, vmem)


def _rope_tables(seq):
    inv = 1.0 / (ROPE_THETA ** (jnp.arange(0, ROPE, 2, dtype=F32) / ROPE))
    ang = jnp.arange(seq, dtype=F32)[:, None] * inv[None, :]
    ang = jnp.concatenate([ang, ang], axis=-1)
    pad = jnp.zeros((seq, 128 - ROPE), F32)
    return (jnp.concatenate([jnp.cos(ang), pad], axis=-1),
            jnp.concatenate([jnp.sin(ang), pad], axis=-1))


def _rot_cols(w):
    half = w.shape[-1] // 2
    return jnp.concatenate([-w[..., half:], w[..., :half]], axis=-1)


def _tile(n, want):
    t = min(n, want)
    while n % t:
        t //= 2
    return t


def _trunk(x3, mem3, p, depth, heads, cc):
    bsz, seq, d = x3.shape
    n_mem = mem3.shape[1]
    x = x3.reshape(bsz * seq, d)
    mem = mem3.reshape(bsz * n_mem, d)
    cosp, sinp = _rope_tables(seq)
    kmem, vmem = _mem_kv(mem, p['g_mem'], p['w_xk'], p['w_xv'], tn=_tile(d, 512))
    tm = _tile(seq, 512)
    tm_ffn = _tile(seq, 1024)
    tm_mix = _tile(seq, 512)
    tm_conv = _tile(seq, 256)
    dff = p['w_ffn1_gate'].shape[-1]
    tf = _tile(dff, 512)
    for l in range(depth):
        x = _ffn(x, l, p['g_ffn1_pre'], p['w_ffn1_gate'], p['w_ffn1_up'], p['w_ffn1_down'],
                 p['g_ffn1_post'], tm=tm_ffn, tf=tf)
        glu, q, k, v = _mix_in(x, l, p['g_mix_pre'], p['w_in_ext'], p['g_q_lat'], p['g_kv_lat'],
                               p['w_qm'], p['w_qr'], p['w_k'], p['w_v'], cosp, sinp,
                               seq=seq, tm=tm_mix, cc=cc, heads=heads)
        conv_n = _conv(glu, l, p['conv_dw_w'], p['conv_dw_b'], p['conv_ln_g'], p['conv_ln_b'],
                       p['g_group_out'], seq=seq, tm=tm_conv, rc=_tile(tm_conv, 64))
        attn = _attn(q, k, v, seq=seq, heads=heads, tq=_tile(seq, 512), tk=_tile(seq, 512))
        x = _proj(conv_n, attn, 0, 0, l, p['g_group_out'], 1, p['w_out'], p['g_mix_post'], x,
                  tm=tm, norm_b=True)
        o = _xattn(x, l, p['g_x_pre'], p['w_xq'], kmem, vmem, seq=seq, n_mem=n_mem, tm=tm)
        x = _proj(o, o, 0, 1, l, p['g_group_out'], 0, p['w_xo'], p['g_x_post'], x,
                  tm=tm, norm_b=False)
        x = _ffn(x, l, p['g_ffn2_pre'], p['w_ffn2_gate'], p['w_ffn2_up'], p['w_ffn2_down'],
                 p['g_ffn2_post'], tm=tm_ffn, tf=tf)
    return x.reshape(bsz, seq, d)


def kernel(x_prompt, x_sample, mem_prompt, mem_sample, g_ffn1_pre, g_ffn1_post, w_ffn1_gate, w_ffn1_up, w_ffn1_down, g_mix_pre, g_mix_post, w_in, conv_dw_w, conv_dw_b, conv_ln_g, conv_ln_b, g_q_lat, g_kv_lat, w_q_up, w_kv_up, g_group_out, w_out, g_x_pre, g_x_post, g_mem, w_xq, w_xk, w_xv, w_xo, g_ffn2_pre, g_ffn2_post, w_ffn2_gate, w_ffn2_up, w_ffn2_down):
    depth, d, _ = w_in.shape
    cc = conv_dw_w.shape[-1]
    ql, kvl = g_q_lat.shape[-1], g_kv_lat.shape[-1]
    heads = w_kv_up.shape[-1] // (NOPE + VDIM)
    o3 = 2 * cc + ql + kvl

    row = lambda g: g[:, None, :]
    p = dict(
        g_ffn1_pre=row(g_ffn1_pre), g_ffn1_post=row(g_ffn1_post),
        g_ffn2_pre=row(g_ffn2_pre), g_ffn2_post=row(g_ffn2_post),
        g_mix_pre=row(g_mix_pre), g_mix_post=row(g_mix_post),
        g_x_pre=row(g_x_pre), g_x_post=row(g_x_post), g_mem=row(g_mem),
        g_q_lat=row(g_q_lat), g_kv_lat=row(g_kv_lat), g_group_out=row(g_group_out),
        conv_dw_w=conv_dw_w, conv_dw_b=row(conv_dw_b), conv_ln_g=row(conv_ln_g),
        conv_ln_b=row(conv_ln_b),
        w_ffn1_gate=w_ffn1_gate.astype(BF), w_ffn1_up=w_ffn1_up.astype(BF),
        w_ffn1_down=w_ffn1_down.astype(BF),
        w_ffn2_gate=w_ffn2_gate.astype(BF), w_ffn2_up=w_ffn2_up.astype(BF),
        w_ffn2_down=w_ffn2_down.astype(BF),
        w_out=w_out.astype(BF), w_xq=w_xq.astype(BF), w_xk=w_xk.astype(BF),
        w_xv=w_xv.astype(BF), w_xo=w_xo.astype(BF),
    )
    w_kr = w_in[:, :, o3:]
    z64 = jnp.zeros((depth, d, 128 - ROPE), w_in.dtype)
    p['w_in_ext'] = jnp.concatenate([w_in[:, :, :o3], w_kr, z64, _rot_cols(w_kr), z64],
                                    axis=-1).astype(BF)
    wq4 = w_q_up.reshape(depth, ql, heads, NOPE + ROPE)
    wq_nope, wq_rope = wq4[..., :NOPE], wq4[..., NOPE:]
    zq = jnp.zeros((depth, ql, heads, 128 - ROPE), w_q_up.dtype)
    p['w_qm'] = jnp.concatenate([wq_rope, zq, wq_nope], axis=-1).reshape(
        depth, ql, heads * HEAD_W).astype(BF)
    p['w_qr'] = jnp.concatenate([_rot_cols(wq_rope), zq], axis=-1).reshape(
        depth, ql, heads * 128).astype(BF)
    wkv4 = w_kv_up.reshape(depth, kvl, heads, NOPE + VDIM)
    p['w_k'] = wkv4[..., :NOPE].reshape(depth, kvl, heads * NOPE).astype(BF)
    p['w_v'] = wkv4[..., NOPE:].reshape(depth, kvl, heads * VDIM).astype(BF)

    y_prompt = _trunk(x_prompt, mem_prompt, p, depth, heads, cc)
    y_sample = _trunk(x_sample, mem_sample, p, depth, heads, cc)
    return (y_prompt, y_sample)
```

```python
import functools

import jax
import jax.numpy as jnp
from jax import lax
from jax.experimental import pallas as pl
from jax.experimental.pallas import tpu as pltpu

EPS = 1e-6
LOG2_E = 1.4426950408889634
ROPE_THETA = 10000.0
NOPE = 128
ROPE = 64
VDIM = 128
HEAD_W = 2 * NOPE
X_HEADS = 4
HALO = 16
SUBLANES = 8
AHEAD = 1
LAG = 2
MAX_EXCESS = 60.0
EDGE_BLOCKS = 4
ROW_GROUP = 4
ONES_ROWS = 16

BF = jnp.bfloat16
F32 = jnp.float32

VMEM_LIMIT = 60 * 1024 * 1024


def _cparams(*sem):
    return pltpu.CompilerParams(dimension_semantics=sem, vmem_limit_bytes=VMEM_LIMIT)


def _rms(x):
    return x * lax.rsqrt(jnp.mean(x * x, axis=-1, keepdims=True) + EPS)


def _dot(a, b):
    return jnp.dot(a, b, preferred_element_type=F32)


def _dot_t(a, b):
    return lax.dot_general(a, b, (((1,), (1,)), ((), ())), preferred_element_type=F32)


def _const_spec(shape, index_map):
    return pl.BlockSpec(shape, index_map, pipeline_mode=pl.Buffered(1))


def _row_blocks(n, parts):
    step = n // parts
    return [slice(j * step, (j + 1) * step) for j in range(parts)]


def _ffn_body(x_ref, gpre_ref, wg_ref, wu_ref, wd_ref, gpost_ref, o_ref, h_ref):
    f = pl.program_id(1)
    last = pl.num_programs(1) - 1
    tm = x_ref.shape[0]

    def step(r, first, final):
        if first:
            h = (_rms(x_ref[r, :]) * gpre_ref[...]).astype(BF)
            h_ref[r, :] = h
        else:
            h = h_ref[r, :]
        a = _dot(h, wg_ref[...])
        b = _dot(h, wu_ref[...])
        z = (a * jax.nn.sigmoid(a) * b).astype(BF)
        acc = _dot(z, wd_ref[...])
        if not first:
            acc = o_ref[r, :] + acc
        if final:
            acc = x_ref[r, :] + 0.5 * (_rms(acc) * gpost_ref[...])
        o_ref[r, :] = acc

    @pl.when(f == 0)
    def _():
        for r in _row_blocks(tm, EDGE_BLOCKS):
            step(r, True, False)

    @pl.when(jnp.logical_and(f > 0, f < last))
    def _():
        step(slice(None), False, False)

    @pl.when(f == last)
    def _():
        for r in _row_blocks(tm, EDGE_BLOCKS):
            step(r, False, True)


def _ffn(x, l, gpre, wg, wu, wd, gpost, *, tm, tf):
    t, d = x.shape
    dff = wg.shape[-1]
    assert dff // tf >= 2 and tm % (16 * EDGE_BLOCKS) == 0
    return pl.pallas_call(
        _ffn_body,
        out_shape=jax.ShapeDtypeStruct((t, d), F32),
        grid=(t // tm, dff // tf),
        in_specs=[
            pl.BlockSpec((tm, d), lambda i, f: (i, 0)),
            pl.BlockSpec((None, 1, d), lambda i, f: (l, 0, 0)),
            pl.BlockSpec((None, d, tf), lambda i, f: (l, 0, f)),
            pl.BlockSpec((None, d, tf), lambda i, f: (l, 0, f)),
            pl.BlockSpec((None, tf, d), lambda i, f: (l, f, 0)),
            pl.BlockSpec((None, 1, d), lambda i, f: (l, 0, 0)),
        ],
        out_specs=pl.BlockSpec((tm, d), lambda i, f: (i, 0)),
        scratch_shapes=[pltpu.VMEM((tm, d), BF)],
        compiler_params=_cparams("parallel", "arbitrary"),
        name="ffn",
    )(x, gpre, wg, wu, wd, gpost)


def _mix_in_body(x_ref, gpre_ref, win_ref, gq_ref, gkv_ref, wqm_ref, wqr_ref, wk_ref, wv_ref,
                 cos_ref, sin_ref, glu_ref, qt_ref, k_ref, vt_ref, *, cc, ql, kvl, heads, scale):
    h = (_rms(x_ref[...]) * gpre_ref[...]).astype(BF)
    gc = min(cc, 512)
    for c in range(0, cc, gc):
        a = _dot(h, win_ref[:, c:c + gc])
        b = _dot(h, win_ref[:, cc + c:cc + c + gc])
        glu_ref[:, c:c + gc] = a * jax.nn.sigmoid(b)
    o1 = 2 * cc
    lat = _dot(h, win_ref[:, o1:])
    qn = (_rms(lat[:, :ql]) * gq_ref[...]).astype(BF)
    kvn = (_rms(lat[:, ql:ql + kvl]) * gkv_ref[...]).astype(BF)
    o3 = ql + kvl
    cosp = cos_ref[...]
    sinp = sin_ref[...]
    krope = (lat[:, o3:o3 + 128] * cosp + lat[:, o3 + 128:o3 + 256] * sinp).astype(BF)
    qm = _dot(qn, wqm_ref[...])
    qr = _dot(qn, wqr_ref[...])
    kn = _dot(kvn, wk_ref[...])
    vt_ref[...] = _dot(kvn, wv_ref[...]).T.astype(BF)
    for hh in range(heads):
        b0 = hh * HEAD_W
        qrope = qm[:, b0:b0 + 128] * cosp + qr[:, hh * 128:(hh + 1) * 128] * sinp
        qt_ref[b0:b0 + 128, :] = (qrope * scale).T.astype(BF)
        qt_ref[b0 + 128:b0 + HEAD_W, :] = (qm[:, b0 + 128:b0 + HEAD_W] * scale).T.astype(BF)
        k_ref[:, b0:b0 + 128] = krope
        k_ref[:, b0 + 128:b0 + HEAD_W] = kn[:, hh * NOPE:(hh + 1) * NOPE].astype(BF)


def _mix_in(x, l, gpre, win, gq, gkv, wqm, wqr, wk, wv, cosp, sinp, *, seq, tm, cc, heads):
    t, d = x.shape
    ql, kvl = gq.shape[-1], gkv.shape[-1]
    nwin = win.shape[-1]
    spt = seq // tm
    body = functools.partial(_mix_in_body, cc=cc, ql=ql, kvl=kvl, heads=heads,
                             scale=float((NOPE + ROPE) ** -0.5 * LOG2_E))
    cst = lambda i: (l, 0, 0)
    return pl.pallas_call(
        body,
        out_shape=(jax.ShapeDtypeStruct((t, cc), F32),
                   jax.ShapeDtypeStruct((heads * HEAD_W, t), BF),
                   jax.ShapeDtypeStruct((t, heads * HEAD_W), BF),
                   jax.ShapeDtypeStruct((heads * VDIM, t), BF)),
        grid=(t // tm,),
        in_specs=[
            pl.BlockSpec((tm, d), lambda i: (i, 0)),
            _const_spec((None, 1, d), cst),
            _const_spec((None, d, nwin), cst),
            _const_spec((None, 1, ql), cst),
            _const_spec((None, 1, kvl), cst),
            _const_spec((None, ql, heads * HEAD_W), cst),
            _const_spec((None, ql, heads * 128), cst),
            _const_spec((None, kvl, heads * NOPE), cst),
            _const_spec((None, kvl, heads * VDIM), cst),
            pl.BlockSpec((tm, 128), lambda i: (i % spt, 0)),
            pl.BlockSpec((tm, 128), lambda i: (i % spt, 0)),
        ],
        out_specs=(pl.BlockSpec((tm, cc), lambda i: (i, 0)),
                   pl.BlockSpec((heads * HEAD_W, tm), lambda i: (0, i)),
                   pl.BlockSpec((tm, heads * HEAD_W), lambda i: (i, 0)),
                   pl.BlockSpec((heads * VDIM, tm), lambda i: (0, i))),
        compiler_params=_cparams("parallel"),
        name="mix_in",
    )(x, gpre, win, gq, gkv, wqm, wqr, wk, wv, cosp, sinp)


def _conv_body(x_ref, prev_ref, next_ref, w_ref, b_ref, lng_ref, lnb_ref, gg_ref, o_ref,
               win_ref, sh_ref, conv_ref, *, tm, kw, spt, rc):
    i = pl.program_id(0)
    first = (i % spt) == 0
    last = (i % spt) == spt - 1
    win_ref[0:HALO, :] = jnp.where(first, 0.0, prev_ref[...])
    win_ref[HALO:HALO + tm, :] = x_ref[...]
    win_ref[HALO + tm:, :] = jnp.where(last, 0.0, next_ref[...])
    cc = x_ref.shape[-1]
    shift = HALO - kw // 2
    nsh = sh_ref.shape[1]
    for b in range(1, SUBLANES):
        sh_ref[b - 1] = win_ref[b:b + nsh, :]

    def tap(k, r, lanes):
        off = shift + k
        a, b = off // SUBLANES, off % SUBLANES
        r0 = r + a * SUBLANES
        if b == 0:
            return win_ref[r0:r0 + rc, lanes]
        return sh_ref[b - 1, r0:r0 + rc, lanes]

    def lane_group(ci, carry):
        lanes = pl.ds(pl.multiple_of(ci * 128, 128), 128)
        bias = b_ref[:, lanes]
        for r0 in range(0, tm, rc * ROW_GROUP):
            rows = range(r0, r0 + rc * ROW_GROUP, rc)
            accs = None
            for k in range(kw):
                wk = w_ref[k:k + 1, lanes]
                prods = [wk * tap(k, r, lanes) for r in rows]
                accs = prods if accs is None else [a + p for a, p in zip(accs, prods)]
            for r, a in zip(rows, accs):
                conv_ref[r:r + rc, lanes] = a + bias
        return carry

    lax.fori_loop(0, cc // 128, lane_group, 0)
    hc = conv_ref[...]
    xc = hc - jnp.mean(hc, axis=-1, keepdims=True)
    y = xc * lax.rsqrt(jnp.mean(xc * xc, axis=-1, keepdims=True) + EPS) * lng_ref[...] + lnb_ref[...]
    s = y * jax.nn.sigmoid(y)
    o_ref[...] = (_rms(s) * gg_ref[...]).astype(BF)


def _conv(glu, l, w, b, lng, lnb, gg, *, seq, tm, rc):
    t, cc = glu.shape
    kw = w.shape[1]
    spt = seq // tm
    hb = tm // HALO
    nhb = t // HALO
    nsh = tm + SUBLANES * ((HALO - kw // 2 + kw - 1) // SUBLANES)
    assert kw // 2 <= HALO and nsh + SUBLANES - 1 <= tm + 2 * HALO
    body = functools.partial(_conv_body, tm=tm, kw=kw, spt=spt, rc=rc)
    cst = lambda i: (l, 0, 0)
    return pl.pallas_call(
        body,
        out_shape=jax.ShapeDtypeStruct((t, cc), BF),
        grid=(t // tm,),
        in_specs=[
            pl.BlockSpec((tm, cc), lambda i: (i, 0)),
            pl.BlockSpec((HALO, cc), lambda i: (jnp.maximum(i * hb - 1, 0), 0)),
            pl.BlockSpec((HALO, cc), lambda i: (jnp.minimum((i + 1) * hb, nhb - 1), 0)),
            pl.BlockSpec((None, kw, cc), cst),
            pl.BlockSpec((None, 1, cc), cst),
            pl.BlockSpec((None, 1, cc), cst),
            pl.BlockSpec((None, 1, cc), cst),
            pl.BlockSpec((None, 1, cc), lambda i: (l, 0, 0)),
        ],
        out_specs=pl.BlockSpec((tm, cc), lambda i: (i, 0)),
        scratch_shapes=[pltpu.VMEM((tm + 2 * HALO, cc), F32),
                        pltpu.VMEM((SUBLANES - 1, nsh, cc), F32),
                        pltpu.VMEM((tm, cc), F32)],
        compiler_params=_cparams("parallel"),
        name="conv",
    )(glu, glu, glu, w, b, lng, lnb, gg)


def _attn_stream(qt, k_ref, vt_ref, *, tk, nk, exact):
    tq = qt.shape[1]
    ones = jnp.ones((ONES_ROWS, tk), BF)
    score = lambda c: _dot(k_ref[c * tk:(c + 1) * tk, :], qt)
    scores = [score(c) for c in range(min(AHEAD, nk))]
    run_max = []
    excess = jnp.zeros((1, tq), F32)
    acc = m_acc = None
    for c in range(nk):
        if c + AHEAD < nk:
            scores.append(score(c + AHEAD))
        st = scores.pop(0)
        cmax = jnp.max(st, axis=0, keepdims=True)
        run_max.append(cmax if c == 0 else jnp.maximum(run_max[-1], cmax))
        if c < exact:
            m_use = run_max[c]
        else:
            m_use = run_max[max(c - LAG, exact - 1)]
            excess = jnp.maximum(excess, cmax - m_use)
        p = jnp.exp2(st - m_use).astype(BF)
        vt_ext = jnp.concatenate([vt_ref[:, c * tk:(c + 1) * tk], ones], axis=0)
        pv = _dot(vt_ext, p)
        acc = pv if c == 0 else jnp.exp2(m_acc - m_use) * acc + pv
        m_acc = m_use
    return acc, excess


def _attn_body(qt_ref, k_ref, vt_ref, o_ref, *, tk, nk):
    qt = qt_ref[...]

    def finish(acc):
        o_ref[...] = (acc[:VDIM] / acc[VDIM:VDIM + 1]).T

    acc, excess = _attn_stream(qt, k_ref, vt_ref, tk=tk, nk=nk, exact=1)
    finish(acc)

    @pl.when(jnp.max(excess) > MAX_EXCESS)
    def _():
        finish(_attn_stream(qt, k_ref, vt_ref, tk=tk, nk=nk, exact=nk)[0])


def _attn(qt, k, vt, *, seq, heads, tq, tk):
    t = k.shape[0]
    bsz = t // seq
    qpt = seq // tq
    body = functools.partial(_attn_body, tk=tk, nk=seq // tk)
    return pl.pallas_call(
        body,
        out_shape=jax.ShapeDtypeStruct((t, heads * VDIM), F32),
        grid=(bsz, heads, qpt),
        in_specs=[
            pl.BlockSpec((HEAD_W, tq), lambda b, h, i: (h, b * qpt + i)),
            pl.BlockSpec((seq, HEAD_W), lambda b, h, i: (b, h)),
            pl.BlockSpec((VDIM, seq), lambda b, h, i: (h, b)),
        ],
        out_specs=pl.BlockSpec((tq, VDIM), lambda b, h, i: (b * qpt + i, h)),
        compiler_params=_cparams("parallel", "parallel", "arbitrary"),
        name="mla_attn",
    )(qt, k, vt)


def _proj_body(a_ref, b_ref, gb_ref, w_ref, gpost_ref, x_ref, o_ref, *, norm_b):
    half = a_ref.shape[-1]
    b = b_ref[...]
    if norm_b:
        b = (_rms(b) * gb_ref[...]).astype(BF)
    t = _dot(a_ref[...], w_ref[:half, :]) + _dot(b, w_ref[half:, :])
    o_ref[...] = x_ref[...] + _rms(t) * gpost_ref[...]


def _proj(a, b, a_col, b_col, l, gb, gb_col, w, gpost, x, *, tm, norm_b):
    t, d = x.shape
    half = w.shape[1] // 2
    body = functools.partial(_proj_body, norm_b=norm_b)
    cst = lambda i: (l, 0, 0)
    return pl.pallas_call(
        body,
        out_shape=jax.ShapeDtypeStruct((t, d), F32),
        grid=(t // tm,),
        in_specs=[
            pl.BlockSpec((tm, half), lambda i: (i, a_col)),
            pl.BlockSpec((tm, half), lambda i: (i, b_col)),
            pl.BlockSpec((None, 1, half), lambda i: (l, 0, gb_col)),
            _const_spec((None, 2 * half, d), cst),
            pl.BlockSpec((None, 1, d), cst),
            pl.BlockSpec((tm, d), lambda i: (i, 0)),
        ],
        out_specs=pl.BlockSpec((tm, d), lambda i: (i, 0)),
        compiler_params=_cparams("parallel"),
        name="proj",
    )(a, b, gb, w, gpost, x)


def _mem_kv_body(m_ref, g_ref, wk_ref, wv_ref, k_ref, v_ref):
    m = (_rms(m_ref[...]) * g_ref[...]).astype(BF)
    k_ref[...] = _dot(m, wk_ref[...]).astype(BF)
    v_ref[...] = _dot(m, wv_ref[...]).astype(BF)


def _mem_kv(mem, g, wk, wv, *, tn):
    rows, d = mem.shape
    depth = wk.shape[0]
    return pl.pallas_call(
        _mem_kv_body,
        out_shape=(jax.ShapeDtypeStruct((depth, rows, d), BF),
                   jax.ShapeDtypeStruct((depth, rows, d), BF)),
        grid=(depth, d // tn),
        in_specs=[
            pl.BlockSpec((rows, d), lambda l, n: (0, 0)),
            pl.BlockSpec((None, 1, d), lambda l, n: (l, 0, 0)),
            pl.BlockSpec((None, d, tn), lambda l, n: (l, 0, n)),
            pl.BlockSpec((None, d, tn), lambda l, n: (l, 0, n)),
        ],
        out_specs=(pl.BlockSpec((None, rows, tn), lambda l, n: (l, 0, n)),
                   pl.BlockSpec((None, rows, tn), lambda l, n: (l, 0, n))),
        compiler_params=_cparams("parallel", "parallel"),
        name="mem_kv",
    )(mem, g, wk, wv)


def _xattn_body(x_ref, g_ref, wq_ref, k_ref, v_ref, o_ref, *, scale):
    h = (_rms(x_ref[...]) * g_ref[...]).astype(BF)
    hd = x_ref.shape[-1] // X_HEADS
    cols = [slice(hh * hd, (hh + 1) * hd) for hh in range(X_HEADS)]
    qproj = lambda hh: _dot(h, wq_ref[:, cols[hh]]).astype(BF)
    q_next = qproj(0)
    for hh in range(X_HEADS):
        cs = cols[hh]
        s = _dot_t(q_next, k_ref[:, cs]) * scale
        if hh + 1 < X_HEADS:
            q_next = qproj(hh + 1)
        e = jnp.exp(s - jnp.max(s, axis=-1, keepdims=True))
        p = e * (1.0 / jnp.sum(e, axis=-1, keepdims=True))
        o_ref[:, cs] = _dot(p.astype(BF), v_ref[:, cs]).astype(BF)


def _xattn(x, l, g, wq, kmem, vmem, *, seq, n_mem, tm):
    t, d = x.shape
    spt = seq // tm
    body = functools.partial(_xattn_body, scale=float((d // X_HEADS) ** -0.5))
    cst = lambda i: (l, 0, 0)
    return pl.pallas_call(
        body,
        out_shape=jax.ShapeDtypeStruct((t, d), BF),
        grid=(t // tm,),
        in_specs=[
            pl.BlockSpec((tm, d), lambda i: (i, 0)),
            pl.BlockSpec((None, 1, d), cst),
            _const_spec((None, d, d), cst),
            pl.BlockSpec((None, n_mem, d), lambda i: (l, i // spt, 0)),
            pl.BlockSpec((None, n_mem, d), lambda i: (l, i // spt, 0)),
        ],
        out_specs=pl.BlockSpec((tm, d), lambda i: (i, 0)),
        compiler_params=_cparams("parallel"),
        name="xattn",
    )(x, g, wq, kmem, vmem)


def _rope_tables(seq):
    inv = 1.0 / (ROPE_THETA ** (jnp.arange(0, ROPE, 2, dtype=F32) / ROPE))
    ang = jnp.arange(seq, dtype=F32)[:, None] * inv[None, :]
    ang = jnp.concatenate([ang, ang], axis=-1)
    pad = jnp.zeros((seq, 128 - ROPE), F32)
    return (jnp.concatenate([jnp.cos(ang), pad], axis=-1),
            jnp.concatenate([jnp.sin(ang), pad], axis=-1))


def _rot_cols(w):
    half = w.shape[-1] // 2
    return jnp.concatenate([-w[..., half:], w[..., :half]], axis=-1)


def _tile(n, want):
    t = min(n, want)
    while n % t:
        t //= 2
    return t


def _trunk(x3, mem3, p, depth, heads, cc):
    bsz, seq, d = x3.shape
    n_mem = mem3.shape[1]
    x = x3.reshape(bsz * seq, d)
    mem = mem3.reshape(bsz * n_mem, d)
    cosp, sinp = _rope_tables(seq)
    kmem, vmem = _mem_kv(mem, p['g_mem'], p['w_xk'], p['w_xv'], tn=_tile(d, 512))
    tm = _tile(seq, 512)
    tm_ffn = _tile(seq, 1024)
    tm_mix = _tile(seq, 512)
    tm_conv = _tile(seq, 512)
    dff = p['w_ffn1_gate'].shape[-1]
    tf = _tile(dff, 512)
    for l in range(depth):
        x = _ffn(x, l, p['g_ffn1_pre'], p['w_ffn1_gate'], p['w_ffn1_up'], p['w_ffn1_down'],
                 p['g_ffn1_post'], tm=tm_ffn, tf=tf)
        glu, q, k, v = _mix_in(x, l, p['g_mix_pre'], p['w_in_ext'], p['g_q_lat'], p['g_kv_lat'],
                               p['w_qm'], p['w_qr'], p['w_k'], p['w_v'], cosp, sinp,
                               seq=seq, tm=tm_mix, cc=cc, heads=heads)
        conv_n = _conv(glu, l, p['conv_dw_w'], p['conv_dw_b'], p['conv_ln_g'], p['conv_ln_b'],
                       p['g_group_out'], seq=seq, tm=tm_conv, rc=_tile(tm_conv // ROW_GROUP, 64))
        attn = _attn(q, k, v, seq=seq, heads=heads, tq=_tile(seq, 512), tk=_tile(seq, 512))
        x = _proj(conv_n, attn, 0, 0, l, p['g_group_out'], 1, p['w_out'], p['g_mix_post'], x,
                  tm=tm, norm_b=True)
        o = _xattn(x, l, p['g_x_pre'], p['w_xq'], kmem, vmem, seq=seq, n_mem=n_mem, tm=tm)
        x = _proj(o, o, 0, 1, l, p['g_group_out'], 0, p['w_xo'], p['g_x_post'], x,
                  tm=tm, norm_b=False)
        x = _ffn(x, l, p['g_ffn2_pre'], p['w_ffn2_gate'], p['w_ffn2_up'], p['w_ffn2_down'],
                 p['g_ffn2_post'], tm=tm_ffn, tf=tf)
    return x.reshape(bsz, seq, d)


def kernel(x_prompt, x_sample, mem_prompt, mem_sample, g_ffn1_pre, g_ffn1_post, w_ffn1_gate, w_ffn1_up, w_ffn1_down, g_mix_pre, g_mix_post, w_in, conv_dw_w, conv_dw_b, conv_ln_g, conv_ln_b, g_q_lat, g_kv_lat, w_q_up, w_kv_up, g_group_out, w_out, g_x_pre, g_x_post, g_mem, w_xq, w_xk, w_xv, w_xo, g_ffn2_pre, g_ffn2_post, w_ffn2_gate, w_ffn2_up, w_ffn2_down):
    depth, d, _ = w_in.shape
    cc = conv_dw_w.shape[-1]
    ql, kvl = g_q_lat.shape[-1], g_kv_lat.shape[-1]
    heads = w_kv_up.shape[-1] // (NOPE + VDIM)
    o3 = 2 * cc + ql + kvl

    row = lambda g: g[:, None, :]
    p = dict(
        g_ffn1_pre=row(g_ffn1_pre), g_ffn1_post=row(g_ffn1_post),
        g_ffn2_pre=row(g_ffn2_pre), g_ffn2_post=row(g_ffn2_post),
        g_mix_pre=row(g_mix_pre), g_mix_post=row(g_mix_post),
        g_x_pre=row(g_x_pre), g_x_post=row(g_x_post), g_mem=row(g_mem),
        g_q_lat=row(g_q_lat), g_kv_lat=row(g_kv_lat), g_group_out=row(g_group_out),
        conv_dw_w=conv_dw_w, conv_dw_b=row(conv_dw_b), conv_ln_g=row(conv_ln_g),
        conv_ln_b=row(conv_ln_b),
        w_ffn1_gate=w_ffn1_gate.astype(BF), w_ffn1_up=w_ffn1_up.astype(BF),
        w_ffn1_down=w_ffn1_down.astype(BF),
        w_ffn2_gate=w_ffn2_gate.astype(BF), w_ffn2_up=w_ffn2_up.astype(BF),
        w_ffn2_down=w_ffn2_down.astype(BF),
        w_out=w_out.astype(BF), w_xq=w_xq.astype(BF), w_xk=w_xk.astype(BF),
        w_xv=w_xv.astype(BF), w_xo=w_xo.astype(BF),
    )
    w_kr = w_in[:, :, o3:]
    z64 = jnp.zeros((depth, d, 128 - ROPE), w_in.dtype)
    p['w_in_ext'] = jnp.concatenate([w_in[:, :, :o3], w_kr, z64, _rot_cols(w_kr), z64],
                                    axis=-1).astype(BF)
    wq4 = w_q_up.reshape(depth, ql, heads, NOPE + ROPE)
    wq_nope, wq_rope = wq4[..., :NOPE], wq4[..., NOPE:]
    zq = jnp.zeros((depth, ql, heads, 128 - ROPE), w_q_up.dtype)
    p['w_qm'] = jnp.concatenate([wq_rope, zq, wq_nope], axis=-1).reshape(
        depth, ql, heads * HEAD_W).astype(BF)
    p['w_qr'] = jnp.concatenate([_rot_cols(wq_rope), zq], axis=-1).reshape(
        depth, ql, heads * 128).astype(BF)
    wkv4 = w_kv_up.reshape(depth, kvl, heads, NOPE + VDIM)
    p['w_k'] = wkv4[..., :NOPE].reshape(depth, kvl, heads * NOPE).astype(BF)
    p['w_v'] = wkv4[..., NOPE:].reshape(depth, kvl, heads * VDIM).astype(BF)

    y_prompt = _trunk(x_prompt, mem_prompt, p, depth, heads, cc)
    y_sample = _trunk(x_sample, mem_sample, p, depth, heads, cc)
    return (y_prompt, y_sample)
```

```python
import functools

import jax
import jax.numpy as jnp
from jax import lax
from jax.experimental import pallas as pl
from jax.experimental.pallas import tpu as pltpu

EPS = 1e-6
LOG2_E = 1.4426950408889634
ROPE_THETA = 10000.0
NOPE = 128
ROPE = 64
VDIM = 128
HEAD_W = 2 * NOPE
X_HEADS = 4
HALO = 16
SUBLANES = 8
AHEAD = 1
LAG = 2
MAX_EXCESS = 60.0
EDGE_BLOCKS = 4
ROW_GROUP = 4
ONES_ROWS = 16

BF = jnp.bfloat16
F32 = jnp.float32

VMEM_LIMIT = 60 * 1024 * 1024


def _cparams(*sem):
    return pltpu.CompilerParams(dimension_semantics=sem, vmem_limit_bytes=VMEM_LIMIT)


def _rms(x):
    return x * lax.rsqrt(jnp.mean(x * x, axis=-1, keepdims=True) + EPS)


def _dot(a, b):
    return jnp.dot(a, b, preferred_element_type=F32)


def _dot_t(a, b):
    return lax.dot_general(a, b, (((1,), (1,)), ((), ())), preferred_element_type=F32)


def _const_spec(shape, index_map):
    return pl.BlockSpec(shape, index_map, pipeline_mode=pl.Buffered(1))


def _row_blocks(n, parts):
    step = n // parts
    return [slice(j * step, (j + 1) * step) for j in range(parts)]


def _ffn_body(x_ref, gpre_ref, wg_ref, wu_ref, wd_ref, gpost_ref, o_ref, h_ref):
    f = pl.program_id(1)
    last = pl.num_programs(1) - 1
    tm = x_ref.shape[0]

    def step(r, first, final):
        if first:
            h = (_rms(x_ref[r, :]) * gpre_ref[...]).astype(BF)
            h_ref[r, :] = h
        else:
            h = h_ref[r, :]
        a = _dot(h, wg_ref[...])
        b = _dot(h, wu_ref[...])
        z = (a * jax.nn.sigmoid(a) * b).astype(BF)
        acc = _dot(z, wd_ref[...])
        if not first:
            acc = o_ref[r, :] + acc
        if final:
            acc = x_ref[r, :] + 0.5 * (_rms(acc) * gpost_ref[...])
        o_ref[r, :] = acc

    @pl.when(f == 0)
    def _():
        for r in _row_blocks(tm, EDGE_BLOCKS):
            step(r, True, False)

    @pl.when(jnp.logical_and(f > 0, f < last))
    def _():
        step(slice(None), False, False)

    @pl.when(f == last)
    def _():
        for r in _row_blocks(tm, EDGE_BLOCKS):
            step(r, False, True)


def _ffn(x, l, gpre, wg, wu, wd, gpost, *, tm, tf):
    t, d = x.shape
    dff = wg.shape[-1]
    assert dff // tf >= 2 and tm % (16 * EDGE_BLOCKS) == 0
    return pl.pallas_call(
        _ffn_body,
        out_shape=jax.ShapeDtypeStruct((t, d), F32),
        grid=(t // tm, dff // tf),
        in_specs=[
            pl.BlockSpec((tm, d), lambda i, f: (i, 0)),
            pl.BlockSpec((None, 1, d), lambda i, f: (l, 0, 0)),
            pl.BlockSpec((None, d, tf), lambda i, f: (l, 0, f)),
            pl.BlockSpec((None, d, tf), lambda i, f: (l, 0, f)),
            pl.BlockSpec((None, tf, d), lambda i, f: (l, f, 0)),
            pl.BlockSpec((None, 1, d), lambda i, f: (l, 0, 0)),
        ],
        out_specs=pl.BlockSpec((tm, d), lambda i, f: (i, 0)),
        scratch_shapes=[pltpu.VMEM((tm, d), BF)],
        compiler_params=_cparams("parallel", "arbitrary"),
        name="ffn",
    )(x, gpre, wg, wu, wd, gpost)


def _mix_in_body(x_ref, gpre_ref, win_ref, gq_ref, gkv_ref, wqm_ref, wqr_ref, wk_ref, wv_ref,
                 cos_ref, sin_ref, glu_ref, qt_ref, k_ref, vt_ref, *, cc, ql, kvl, heads, scale):
    h = (_rms(x_ref[...]) * gpre_ref[...]).astype(BF)
    gc = min(cc, 512)
    for c in range(0, cc, gc):
        a = _dot(h, win_ref[:, c:c + gc])
        b = _dot(h, win_ref[:, cc + c:cc + c + gc])
        glu_ref[:, c:c + gc] = a * jax.nn.sigmoid(b)
    o1 = 2 * cc
    lat = _dot(h, win_ref[:, o1:])
    qn = (_rms(lat[:, :ql]) * gq_ref[...]).astype(BF)
    kvn = (_rms(lat[:, ql:ql + kvl]) * gkv_ref[...]).astype(BF)
    o3 = ql + kvl
    cosp = cos_ref[...]
    sinp = sin_ref[...]
    krope = (lat[:, o3:o3 + 128] * cosp + lat[:, o3 + 128:o3 + 256] * sinp).astype(BF)
    qm = _dot(qn, wqm_ref[...])
    qr = _dot(qn, wqr_ref[...])
    kn = _dot(kvn, wk_ref[...])
    vt_ref[...] = _dot(kvn, wv_ref[...]).T.astype(BF)
    for hh in range(heads):
        b0 = hh * HEAD_W
        qrope = qm[:, b0:b0 + 128] * cosp + qr[:, hh * 128:(hh + 1) * 128] * sinp
        qt_ref[b0:b0 + 128, :] = (qrope * scale).T.astype(BF)
        qt_ref[b0 + 128:b0 + HEAD_W, :] = (qm[:, b0 + 128:b0 + HEAD_W] * scale).T.astype(BF)
        k_ref[:, b0:b0 + 128] = krope
        k_ref[:, b0 + 128:b0 + HEAD_W] = kn[:, hh * NOPE:(hh + 1) * NOPE].astype(BF)


def _mix_in(x, l, gpre, win, gq, gkv, wqm, wqr, wk, wv, cosp, sinp, *, seq, tm, cc, heads):
    t, d = x.shape
    ql, kvl = gq.shape[-1], gkv.shape[-1]
    nwin = win.shape[-1]
    spt = seq // tm
    body = functools.partial(_mix_in_body, cc=cc, ql=ql, kvl=kvl, heads=heads,
                             scale=float((NOPE + ROPE) ** -0.5 * LOG2_E))
    cst = lambda i: (l, 0, 0)
    return pl.pallas_call(
        body,
        out_shape=(jax.ShapeDtypeStruct((t, cc), F32),
                   jax.ShapeDtypeStruct((heads * HEAD_W, t), BF),
                   jax.ShapeDtypeStruct((t, heads * HEAD_W), BF),
                   jax.ShapeDtypeStruct((heads * VDIM, t), BF)),
        grid=(t // tm,),
        in_specs=[
            pl.BlockSpec((tm, d), lambda i: (i, 0)),
            _const_spec((None, 1, d), cst),
            _const_spec((None, d, nwin), cst),
            _const_spec((None, 1, ql), cst),
            _const_spec((None, 1, kvl), cst),
            _const_spec((None, ql, heads * HEAD_W), cst),
            _const_spec((None, ql, heads * 128), cst),
            _const_spec((None, kvl, heads * NOPE), cst),
            _const_spec((None, kvl, heads * VDIM), cst),
            pl.BlockSpec((tm, 128), lambda i: (i % spt, 0)),
            pl.BlockSpec((tm, 128), lambda i: (i % spt, 0)),
        ],
        out_specs=(pl.BlockSpec((tm, cc), lambda i: (i, 0)),
                   pl.BlockSpec((heads * HEAD_W, tm), lambda i: (0, i)),
                   pl.BlockSpec((tm, heads * HEAD_W), lambda i: (i, 0)),
                   pl.BlockSpec((heads * VDIM, tm), lambda i: (0, i))),
        compiler_params=_cparams("parallel"),
        name="mix_in",
    )(x, gpre, win, gq, gkv, wqm, wqr, wk, wv, cosp, sinp)


def _conv_body(x_ref, prev_ref, next_ref, w_ref, b_ref, lng_ref, lnb_ref, gg_ref, o_ref,
               win_ref, sh_ref, conv_ref, *, tm, kw, spt, rc):
    i = pl.program_id(0)
    first = (i % spt) == 0
    last = (i % spt) == spt - 1
    win_ref[0:HALO, :] = jnp.where(first, 0.0, prev_ref[...])
    win_ref[HALO:HALO + tm, :] = x_ref[...]
    win_ref[HALO + tm:, :] = jnp.where(last, 0.0, next_ref[...])
    cc = x_ref.shape[-1]
    shift = HALO - kw // 2
    nsh = sh_ref.shape[1]
    for b in range(1, SUBLANES):
        sh_ref[b - 1] = win_ref[b:b + nsh, :]

    def tap(k, r, lanes):
        off = shift + k
        a, b = off // SUBLANES, off % SUBLANES
        r0 = r + a * SUBLANES
        if b == 0:
            return win_ref[r0:r0 + rc, lanes]
        return sh_ref[b - 1, r0:r0 + rc, lanes]

    def lane_group(ci, carry):
        lanes = pl.ds(pl.multiple_of(ci * 128, 128), 128)
        bias = b_ref[:, lanes]
        for r0 in range(0, tm, rc * ROW_GROUP):
            rows = range(r0, r0 + rc * ROW_GROUP, rc)
            accs = None
            for k in range(kw):
                wk = w_ref[k:k + 1, lanes]
                prods = [wk * tap(k, r, lanes) for r in rows]
                accs = prods if accs is None else [a + p for a, p in zip(accs, prods)]
            for r, a in zip(rows, accs):
                conv_ref[r:r + rc, lanes] = a + bias
        return carry

    lax.fori_loop(0, cc // 128, lane_group, 0)
    hc = conv_ref[...]
    xc = hc - jnp.mean(hc, axis=-1, keepdims=True)
    y = xc * lax.rsqrt(jnp.mean(xc * xc, axis=-1, keepdims=True) + EPS) * lng_ref[...] + lnb_ref[...]
    s = y * jax.nn.sigmoid(y)
    o_ref[...] = (_rms(s) * gg_ref[...]).astype(BF)


def _conv(glu, l, w, b, lng, lnb, gg, *, seq, tm, rc):
    t, cc = glu.shape
    kw = w.shape[1]
    spt = seq // tm
    hb = tm // HALO
    nhb = t // HALO
    nsh = tm + SUBLANES * ((HALO - kw // 2 + kw - 1) // SUBLANES)
    assert kw // 2 <= HALO and nsh + SUBLANES - 1 <= tm + 2 * HALO
    body = functools.partial(_conv_body, tm=tm, kw=kw, spt=spt, rc=rc)
    cst = lambda i: (l, 0, 0)
    return pl.pallas_call(
        body,
        out_shape=jax.ShapeDtypeStruct((t, cc), BF),
        grid=(t // tm,),
        in_specs=[
            pl.BlockSpec((tm, cc), lambda i: (i, 0)),
            pl.BlockSpec((HALO, cc), lambda i: (jnp.maximum(i * hb - 1, 0), 0)),
            pl.BlockSpec((HALO, cc), lambda i: (jnp.minimum((i + 1) * hb, nhb - 1), 0)),
            pl.BlockSpec((None, kw, cc), cst),
            pl.BlockSpec((None, 1, cc), cst),
            pl.BlockSpec((None, 1, cc), cst),
            pl.BlockSpec((None, 1, cc), cst),
            pl.BlockSpec((None, 1, cc), lambda i: (l, 0, 0)),
        ],
        out_specs=pl.BlockSpec((tm, cc), lambda i: (i, 0)),
        scratch_shapes=[pltpu.VMEM((tm + 2 * HALO, cc), F32),
                        pltpu.VMEM((SUBLANES - 1, nsh, cc), F32),
                        pltpu.VMEM((tm, cc), F32)],
        compiler_params=_cparams("parallel"),
        name="conv",
    )(glu, glu, glu, w, b, lng, lnb, gg)


def _attn_stream(qt, k_ref, vt_ref, *, tk, nk, exact):
    tq = qt.shape[1]
    ones = jnp.ones((ONES_ROWS, tk), BF)
    score = lambda c: _dot(k_ref[c * tk:(c + 1) * tk, :], qt)
    scores = [score(c) for c in range(min(AHEAD, nk))]
    run_max = []
    excess = jnp.zeros((1, tq), F32)
    acc = m_acc = None
    for c in range(nk):
        if c + AHEAD < nk:
            scores.append(score(c + AHEAD))
        st = scores.pop(0)
        cmax = jnp.max(st, axis=0, keepdims=True)
        run_max.append(cmax if c == 0 else jnp.maximum(run_max[-1], cmax))
        if c < exact:
            m_use = run_max[c]
        else:
            m_use = run_max[max(c - LAG, exact - 1)]
            excess = jnp.maximum(excess, cmax - m_use)
        p = jnp.exp2(st - m_use).astype(BF)
        vt_ext = jnp.concatenate([vt_ref[:, c * tk:(c + 1) * tk], ones], axis=0)
        pv = _dot(vt_ext, p)
        acc = pv if c == 0 else jnp.exp2(m_acc - m_use) * acc + pv
        m_acc = m_use
    return acc, excess


def _attn_body(qt_ref, k_ref, vt_ref, o_ref, *, tk, nk):
    qt = qt_ref[...]

    def finish(acc):
        o_ref[...] = (acc[:VDIM] / acc[VDIM:VDIM + 1]).T

    acc, excess = _attn_stream(qt, k_ref, vt_ref, tk=tk, nk=nk, exact=1)
    finish(acc)

    @pl.when(jnp.max(excess) > MAX_EXCESS)
    def _():
        finish(_attn_stream(qt, k_ref, vt_ref, tk=tk, nk=nk, exact=nk)[0])


def _attn(qt, k, vt, *, seq, heads, tq, tk):
    t = k.shape[0]
    bsz = t // seq
    qpt = seq // tq
    body = functools.partial(_attn_body, tk=tk, nk=seq // tk)
    return pl.pallas_call(
        body,
        out_shape=jax.ShapeDtypeStruct((t, heads * VDIM), F32),
        grid=(bsz, heads, qpt),
        in_specs=[
            pl.BlockSpec((HEAD_W, tq), lambda b, h, i: (h, b * qpt + i)),
            pl.BlockSpec((seq, HEAD_W), lambda b, h, i: (b, h)),
            pl.BlockSpec((VDIM, seq), lambda b, h, i: (h, b)),
        ],
        out_specs=pl.BlockSpec((tq, VDIM), lambda b, h, i: (b * qpt + i, h)),
        compiler_params=_cparams("parallel", "parallel", "arbitrary"),
        name="mla_attn",
    )(qt, k, vt)


def _proj_body(a_ref, b_ref, gb_ref, w_ref, gpost_ref, x_ref, o_ref, *, norm_b):
    half = a_ref.shape[-1]
    b = b_ref[...]
    if norm_b:
        b = (_rms(b) * gb_ref[...]).astype(BF)
    t = _dot(a_ref[...], w_ref[:half, :]) + _dot(b, w_ref[half:, :])
    o_ref[...] = x_ref[...] + _rms(t) * gpost_ref[...]


def _proj(a, b, a_col, b_col, l, gb, gb_col, w, gpost, x, *, tm, norm_b):
    t, d = x.shape
    half = w.shape[1] // 2
    body = functools.partial(_proj_body, norm_b=norm_b)
    cst = lambda i: (l, 0, 0)
    return pl.pallas_call(
        body,
        out_shape=jax.ShapeDtypeStruct((t, d), F32),
        grid=(t // tm,),
        in_specs=[
            pl.BlockSpec((tm, half), lambda i: (i, a_col)),
            pl.BlockSpec((tm, half), lambda i: (i, b_col)),
            pl.BlockSpec((None, 1, half), lambda i: (l, 0, gb_col)),
            _const_spec((None, 2 * half, d), cst),
            pl.BlockSpec((None, 1, d), cst),
            pl.BlockSpec((tm, d), lambda i: (i, 0)),
        ],
        out_specs=pl.BlockSpec((tm, d), lambda i: (i, 0)),
        compiler_params=_cparams("parallel"),
        name="proj",
    )(a, b, gb, w, gpost, x)


def _mem_kv_body(m_ref, g_ref, wk_ref, wv_ref, k_ref, v_ref):
    m = (_rms(m_ref[...]) * g_ref[...]).astype(BF)
    k_ref[...] = _dot(m, wk_ref[...]).astype(BF)
    v_ref[...] = _dot(m, wv_ref[...]).astype(BF)


def _mem_kv(mem, g, wk, wv, *, tn):
    rows, d = mem.shape
    depth = wk.shape[0]
    return pl.pallas_call(
        _mem_kv_body,
        out_shape=(jax.ShapeDtypeStruct((depth, rows, d), BF),
                   jax.ShapeDtypeStruct((depth, rows, d), BF)),
        grid=(depth, d // tn),
        in_specs=[
            pl.BlockSpec((rows, d), lambda l, n: (0, 0)),
            pl.BlockSpec((None, 1, d), lambda l, n: (l, 0, 0)),
            pl.BlockSpec((None, d, tn), lambda l, n: (l, 0, n)),
            pl.BlockSpec((None, d, tn), lambda l, n: (l, 0, n)),
        ],
        out_specs=(pl.BlockSpec((None, rows, tn), lambda l, n: (l, 0, n)),
                   pl.BlockSpec((None, rows, tn), lambda l, n: (l, 0, n))),
        compiler_params=_cparams("parallel", "parallel"),
        name="mem_kv",
    )(mem, g, wk, wv)


def _xattn_body(x_ref, g_ref, wq_ref, k_ref, v_ref, o_ref, *, scale):
    h = (_rms(x_ref[...]) * g_ref[...]).astype(BF)
    hd = x_ref.shape[-1] // X_HEADS
    cols = [slice(hh * hd, (hh + 1) * hd) for hh in range(X_HEADS)]
    qproj = lambda hh: _dot(h, wq_ref[:, cols[hh]]).astype(BF)
    q_next = qproj(0)
    for hh in range(X_HEADS):
        cs = cols[hh]
        s = _dot_t(q_next, k_ref[:, cs]) * scale
        if hh + 1 < X_HEADS:
            q_next = qproj(hh + 1)
        e = jnp.exp(s - jnp.max(s, axis=-1, keepdims=True))
        p = e * (1.0 / jnp.sum(e, axis=-1, keepdims=True))
        o_ref[:, cs] = _dot(p.astype(BF), v_ref[:, cs]).astype(BF)


def _xattn(x, l, g, wq, kmem, vmem, *, seq, n_mem, tm):
    t, d = x.shape
    spt = seq // tm
    body = functools.partial(_xattn_body, scale=float((d // X_HEADS) ** -0.5))
    cst = lambda i: (l, 0, 0)
    return pl.pallas_call(
        body,
        out_shape=jax.ShapeDtypeStruct((t, d), BF),
        grid=(t // tm,),
        in_specs=[
            pl.BlockSpec((tm, d), lambda i: (i, 0)),
            pl.BlockSpec((None, 1, d), cst),
            _const_spec((None, d, d), cst),
            pl.BlockSpec((None, n_mem, d), lambda i: (l, i // spt, 0)),
            pl.BlockSpec((None, n_mem, d), lambda i: (l, i // spt, 0)),
        ],
        out_specs=pl.BlockSpec((tm, d), lambda i: (i, 0)),
        compiler_params=_cparams("parallel"),
        name="xattn",
    )(x, g, wq, kmem, vmem)


def _rope_tables(seq):
    inv = 1.0 / (ROPE_THETA ** (jnp.arange(0, ROPE, 2, dtype=F32) / ROPE))
    ang = jnp.arange(seq, dtype=F32)[:, None] * inv[None, :]
    ang = jnp.concatenate([ang, ang], axis=-1)
    pad = jnp.zeros((seq, 128 - ROPE), F32)
    return (jnp.concatenate([jnp.cos(ang), pad], axis=-1),
            jnp.concatenate([jnp.sin(ang), pad], axis=-1))


def _rot_cols(w):
    half = w.shape[-1] // 2
    return jnp.concatenate([-w[..., half:], w[..., :half]], axis=-1)


def _tile(n, want):
    t = min(n, want)
    while n % t:
        t //= 2
    return t


def _trunk(x3, mem3, p, depth, heads, cc):
    bsz, seq, d = x3.shape
    n_mem = mem3.shape[1]
    x = x3.reshape(bsz * seq, d)
    mem = mem3.reshape(bsz * n_mem, d)
    cosp, sinp = _rope_tables(seq)
    kmem, vmem = _mem_kv(mem, p['g_mem'], p['w_xk'], p['w_xv'], tn=_tile(d, 512))
    tm = _tile(seq, 512)
    tm_ffn = _tile(seq, 1024)
    tm_mix = _tile(seq, 512)
    tm_conv = _tile(seq, 512)
    dff = p['w_ffn1_gate'].shape[-1]
    tf = _tile(dff, 512)
    for l in range(depth):
        x = _ffn(x, l, p['g_ffn1_pre'], p['w_ffn1_gate'], p['w_ffn1_up'], p['w_ffn1_down'],
                 p['g_ffn1_post'], tm=tm_ffn, tf=tf)
        glu, q, k, v = _mix_in(x, l, p['g_mix_pre'], p['w_in_ext'], p['g_q_lat'], p['g_kv_lat'],
                               p['w_qm'], p['w_qr'], p['w_k'], p['w_v'], cosp, sinp,
                               seq=seq, tm=tm_mix, cc=cc, heads=heads)
        conv_n = _conv(glu, l, p['conv_dw_w'], p['conv_dw_b'], p['conv_ln_g'], p['conv_ln_b'],
                       p['g_group_out'], seq=seq, tm=tm_conv, rc=_tile(tm_conv // ROW_GROUP, 64))
        attn = _attn(q, k, v, seq=seq, heads=heads, tq=_tile(seq, 1024), tk=_tile(seq, 512))
        x = _proj(conv_n, attn, 0, 0, l, p['g_group_out'], 1, p['w_out'], p['g_mix_post'], x,
                  tm=tm, norm_b=True)
        o = _xattn(x, l, p['g_x_pre'], p['w_xq'], kmem, vmem, seq=seq, n_mem=n_mem, tm=tm)
        x = _proj(o, o, 0, 1, l, p['g_group_out'], 0, p['w_xo'], p['g_x_post'], x,
                  tm=tm, norm_b=False)
        x = _ffn(x, l, p['g_ffn2_pre'], p['w_ffn2_gate'], p['w_ffn2_up'], p['w_ffn2_down'],
                 p['g_ffn2_post'], tm=tm_ffn, tf=tf)
    return x.reshape(bsz, seq, d)


def kernel(x_prompt, x_sample, mem_prompt, mem_sample, g_ffn1_pre, g_ffn1_post, w_ffn1_gate, w_ffn1_up, w_ffn1_down, g_mix_pre, g_mix_post, w_in, conv_dw_w, conv_dw_b, conv_ln_g, conv_ln_b, g_q_lat, g_kv_lat, w_q_up, w_kv_up, g_group_out, w_out, g_x_pre, g_x_post, g_mem, w_xq, w_xk, w_xv, w_xo, g_ffn2_pre, g_ffn2_post, w_ffn2_gate, w_ffn2_up, w_ffn2_down):
    depth, d, _ = w_in.shape
    cc = conv_dw_w.shape[-1]
    ql, kvl = g_q_lat.shape[-1], g_kv_lat.shape[-1]
    heads = w_kv_up.shape[-1] // (NOPE + VDIM)
    o3 = 2 * cc + ql + kvl

    row = lambda g: g[:, None, :]
    p = dict(
        g_ffn1_pre=row(g_ffn1_pre), g_ffn1_post=row(g_ffn1_post),
        g_ffn2_pre=row(g_ffn2_pre), g_ffn2_post=row(g_ffn2_post),
        g_mix_pre=row(g_mix_pre), g_mix_post=row(g_mix_post),
        g_x_pre=row(g_x_pre), g_x_post=row(g_x_post), g_mem=row(g_mem),
        g_q_lat=row(g_q_lat), g_kv_lat=row(g_kv_lat), g_group_out=row(g_group_out),
        conv_dw_w=conv_dw_w, conv_dw_b=row(conv_dw_b), conv_ln_g=row(conv_ln_g),
        conv_ln_b=row(conv_ln_b),
        w_ffn1_gate=w_ffn1_gate.astype(BF), w_ffn1_up=w_ffn1_up.astype(BF),
        w_ffn1_down=w_ffn1_down.astype(BF),
        w_ffn2_gate=w_ffn2_gate.astype(BF), w_ffn2_up=w_ffn2_up.astype(BF),
        w_ffn2_down=w_ffn2_down.astype(BF),
        w_out=w_out.astype(BF), w_xq=w_xq.astype(BF), w_xk=w_xk.astype(BF),
        w_xv=w_xv.astype(BF), w_xo=w_xo.astype(BF),
    )
    w_kr = w_in[:, :, o3:]
    z64 = jnp.zeros((depth, d, 128 - ROPE), w_in.dtype)
    p['w_in_ext'] = jnp.concatenate([w_in[:, :, :o3], w_kr, z64, _rot_cols(w_kr), z64],
                                    axis=-1).astype(BF)
    wq4 = w_q_up.reshape(depth, ql, heads, NOPE + ROPE)
    wq_nope, wq_rope = wq4[..., :NOPE], wq4[..., NOPE:]
    zq = jnp.zeros((depth, ql, heads, 128 - ROPE), w_q_up.dtype)
    p['w_qm'] = jnp.concatenate([wq_rope, zq, wq_nope], axis=-1).reshape(
        depth, ql, heads * HEAD_W).astype(BF)
    p['w_qr'] = jnp.concatenate([_rot_cols(wq_rope), zq], axis=-1).reshape(
        depth, ql, heads * 128).astype(BF)
    wkv4 = w_kv_up.reshape(depth, kvl, heads, NOPE + VDIM)
    p['w_k'] = wkv4[..., :NOPE].reshape(depth, kvl, heads * NOPE).astype(BF)
    p['w_v'] = wkv4[..., NOPE:].reshape(depth, kvl, heads * VDIM).astype(BF)

    y_prompt = _trunk(x_prompt, mem_prompt, p, depth, heads, cc)
    y_sample = _trunk(x_sample, mem_sample, p, depth, heads, cc)
    return (y_prompt, y_sample)
```

```python
import functools

import jax
import jax.numpy as jnp
from jax import lax
from jax.experimental import pallas as pl
from jax.experimental.pallas import tpu as pltpu

EPS = 1e-6
LOG2_E = 1.4426950408889634
ROPE_THETA = 10000.0
NOPE = 128
ROPE = 64
VDIM = 128
HEAD_W = 2 * NOPE
X_HEADS = 4
HALO = 16
SUBLANES = 8
AHEAD = 1
LAG = 2
MAX_EXCESS = 60.0
EDGE_BLOCKS = 4
ROW_GROUP = 4
ONES_ROWS = 16

BF = jnp.bfloat16
F32 = jnp.float32

VMEM_LIMIT = 60 * 1024 * 1024


def _cparams(*sem):
    return pltpu.CompilerParams(dimension_semantics=sem, vmem_limit_bytes=VMEM_LIMIT)


def _rms(x):
    return x * lax.rsqrt(jnp.mean(x * x, axis=-1, keepdims=True) + EPS)


def _dot(a, b):
    return jnp.dot(a, b, preferred_element_type=F32)


def _dot_t(a, b):
    return lax.dot_general(a, b, (((1,), (1,)), ((), ())), preferred_element_type=F32)


def _const_spec(shape, index_map):
    return pl.BlockSpec(shape, index_map, pipeline_mode=pl.Buffered(1))


def _row_blocks(n, parts):
    step = n // parts
    return [slice(j * step, (j + 1) * step) for j in range(parts)]


def _ffn_body(x_ref, gpre_ref, wg_ref, wu_ref, wd_ref, gpost_ref, o_ref, h_ref):
    f = pl.program_id(1)
    last = pl.num_programs(1) - 1
    tm = x_ref.shape[0]

    def step(r, first, final):
        if first:
            h = (_rms(x_ref[r, :]) * gpre_ref[...]).astype(BF)
            h_ref[r, :] = h
        else:
            h = h_ref[r, :]
        a = _dot(h, wg_ref[...])
        b = _dot(h, wu_ref[...])
        z = (a * jax.nn.sigmoid(a) * b).astype(BF)
        acc = _dot(z, wd_ref[...])
        if not first:
            acc = o_ref[r, :] + acc
        if final:
            acc = x_ref[r, :] + 0.5 * (_rms(acc) * gpost_ref[...])
        o_ref[r, :] = acc

    @pl.when(f == 0)
    def _():
        for r in _row_blocks(tm, EDGE_BLOCKS):
            step(r, True, False)

    @pl.when(jnp.logical_and(f > 0, f < last))
    def _():
        step(slice(None), False, False)

    @pl.when(f == last)
    def _():
        for r in _row_blocks(tm, EDGE_BLOCKS):
            step(r, False, True)


def _ffn(x, l, gpre, wg, wu, wd, gpost, *, tm, tf):
    t, d = x.shape
    dff = wg.shape[-1]
    assert dff // tf >= 2 and tm % (16 * EDGE_BLOCKS) == 0
    return pl.pallas_call(
        _ffn_body,
        out_shape=jax.ShapeDtypeStruct((t, d), F32),
        grid=(t // tm, dff // tf),
        in_specs=[
            pl.BlockSpec((tm, d), lambda i, f: (i, 0)),
            pl.BlockSpec((None, 1, d), lambda i, f: (l, 0, 0)),
            pl.BlockSpec((None, d, tf), lambda i, f: (l, 0, f)),
            pl.BlockSpec((None, d, tf), lambda i, f: (l, 0, f)),
            pl.BlockSpec((None, tf, d), lambda i, f: (l, f, 0)),
            pl.BlockSpec((None, 1, d), lambda i, f: (l, 0, 0)),
        ],
        out_specs=pl.BlockSpec((tm, d), lambda i, f: (i, 0)),
        scratch_shapes=[pltpu.VMEM((tm, d), BF)],
        compiler_params=_cparams("parallel", "arbitrary"),
        name="ffn",
    )(x, gpre, wg, wu, wd, gpost)


def _mix_in_body(x_ref, gpre_ref, win_ref, gq_ref, gkv_ref, wqm_ref, wqr_ref, wk_ref, wv_ref,
                 cos_ref, sin_ref, glu_ref, qt_ref, k_ref, vt_ref, *, cc, ql, kvl, heads, scale):
    h = (_rms(x_ref[...]) * gpre_ref[...]).astype(BF)
    gc = min(cc, 512)
    for c in range(0, cc, gc):
        a = _dot(h, win_ref[:, c:c + gc])
        b = _dot(h, win_ref[:, cc + c:cc + c + gc])
        glu_ref[:, c:c + gc] = a * jax.nn.sigmoid(b)
    o1 = 2 * cc
    lat = _dot(h, win_ref[:, o1:])
    qn = (_rms(lat[:, :ql]) * gq_ref[...]).astype(BF)
    kvn = (_rms(lat[:, ql:ql + kvl]) * gkv_ref[...]).astype(BF)
    o3 = ql + kvl
    cosp = cos_ref[...]
    sinp = sin_ref[...]
    krope = (lat[:, o3:o3 + 128] * cosp + lat[:, o3 + 128:o3 + 256] * sinp).astype(BF)
    qm = _dot(qn, wqm_ref[...])
    qr = _dot(qn, wqr_ref[...])
    kn = _dot(kvn, wk_ref[...])
    vt_ref[...] = _dot(kvn, wv_ref[...]).T.astype(BF)
    for hh in range(heads):
        b0 = hh * HEAD_W
        qrope = qm[:, b0:b0 + 128] * cosp + qr[:, hh * 128:(hh + 1) * 128] * sinp
        qt_ref[b0:b0 + 128, :] = (qrope * scale).T.astype(BF)
        qt_ref[b0 + 128:b0 + HEAD_W, :] = (qm[:, b0 + 128:b0 + HEAD_W] * scale).T.astype(BF)
        k_ref[:, b0:b0 + 128] = krope
        k_ref[:, b0 + 128:b0 + HEAD_W] = kn[:, hh * NOPE:(hh + 1) * NOPE].astype(BF)


def _mix_in(x, l, gpre, win, gq, gkv, wqm, wqr, wk, wv, cosp, sinp, *, seq, tm, cc, heads):
    t, d = x.shape
    ql, kvl = gq.shape[-1], gkv.shape[-1]
    nwin = win.shape[-1]
    spt = seq // tm
    body = functools.partial(_mix_in_body, cc=cc, ql=ql, kvl=kvl, heads=heads,
                             scale=float((NOPE + ROPE) ** -0.5 * LOG2_E))
    cst = lambda i: (l, 0, 0)
    return pl.pallas_call(
        body,
        out_shape=(jax.ShapeDtypeStruct((t, cc), F32),
                   jax.ShapeDtypeStruct((heads * HEAD_W, t), BF),
                   jax.ShapeDtypeStruct((t, heads * HEAD_W), BF),
                   jax.ShapeDtypeStruct((heads * VDIM, t), BF)),
        grid=(t // tm,),
        in_specs=[
            pl.BlockSpec((tm, d), lambda i: (i, 0)),
            _const_spec((None, 1, d), cst),
            _const_spec((None, d, nwin), cst),
            _const_spec((None, 1, ql), cst),
            _const_spec((None, 1, kvl), cst),
            _const_spec((None, ql, heads * HEAD_W), cst),
            _const_spec((None, ql, heads * 128), cst),
            _const_spec((None, kvl, heads * NOPE), cst),
            _const_spec((None, kvl, heads * VDIM), cst),
            pl.BlockSpec((tm, 128), lambda i: (i % spt, 0)),
            pl.BlockSpec((tm, 128), lambda i: (i % spt, 0)),
        ],
        out_specs=(pl.BlockSpec((tm, cc), lambda i: (i, 0)),
                   pl.BlockSpec((heads * HEAD_W, tm), lambda i: (0, i)),
                   pl.BlockSpec((tm, heads * HEAD_W), lambda i: (i, 0)),
                   pl.BlockSpec((heads * VDIM, tm), lambda i: (0, i))),
        compiler_params=_cparams("parallel"),
        name="mix_in",
    )(x, gpre, win, gq, gkv, wqm, wqr, wk, wv, cosp, sinp)


def _conv_body(x_ref, prev_ref, next_ref, w_ref, b_ref, lng_ref, lnb_ref, gg_ref, o_ref,
               win_ref, sh_ref, conv_ref, *, tm, kw, spt, rc):
    i = pl.program_id(0)
    first = (i % spt) == 0
    last = (i % spt) == spt - 1
    win_ref[0:HALO, :] = jnp.where(first, 0.0, prev_ref[...])
    win_ref[HALO:HALO + tm, :] = x_ref[...]
    win_ref[HALO + tm:, :] = jnp.where(last, 0.0, next_ref[...])
    cc = x_ref.shape[-1]
    shift = HALO - kw // 2
    nsh = sh_ref.shape[1]
    for b in range(1, SUBLANES):
        sh_ref[b - 1] = win_ref[b:b + nsh, :]

    def tap(k, r, lanes):
        off = shift + k
        a, b = off // SUBLANES, off % SUBLANES
        r0 = r + a * SUBLANES
        if b == 0:
            return win_ref[r0:r0 + rc, lanes]
        return sh_ref[b - 1, r0:r0 + rc, lanes]

    def lane_group(ci, carry):
        lanes = pl.ds(pl.multiple_of(ci * 128, 128), 128)
        bias = b_ref[:, lanes]
        for r0 in range(0, tm, rc * ROW_GROUP):
            rows = range(r0, r0 + rc * ROW_GROUP, rc)
            accs = None
            for k in range(kw):
                wk = w_ref[k:k + 1, lanes]
                prods = [wk * tap(k, r, lanes) for r in rows]
                accs = prods if accs is None else [a + p for a, p in zip(accs, prods)]
            for r, a in zip(rows, accs):
                conv_ref[r:r + rc, lanes] = a + bias
        return carry

    lax.fori_loop(0, cc // 128, lane_group, 0)
    hc = conv_ref[...]
    xc = hc - jnp.mean(hc, axis=-1, keepdims=True)
    y = xc * lax.rsqrt(jnp.mean(xc * xc, axis=-1, keepdims=True) + EPS) * lng_ref[...] + lnb_ref[...]
    s = y * jax.nn.sigmoid(y)
    o_ref[...] = (_rms(s) * gg_ref[...]).astype(BF)


def _conv(glu, l, w, b, lng, lnb, gg, *, seq, tm, rc):
    t, cc = glu.shape
    kw = w.shape[1]
    spt = seq // tm
    hb = tm // HALO
    nhb = t // HALO
    nsh = tm + SUBLANES * ((HALO - kw // 2 + kw - 1) // SUBLANES)
    assert kw // 2 <= HALO and nsh + SUBLANES - 1 <= tm + 2 * HALO
    body = functools.partial(_conv_body, tm=tm, kw=kw, spt=spt, rc=rc)
    cst = lambda i: (l, 0, 0)
    return pl.pallas_call(
        body,
        out_shape=jax.ShapeDtypeStruct((t, cc), BF),
        grid=(t // tm,),
        in_specs=[
            pl.BlockSpec((tm, cc), lambda i: (i, 0)),
            pl.BlockSpec((HALO, cc), lambda i: (jnp.maximum(i * hb - 1, 0), 0)),
            pl.BlockSpec((HALO, cc), lambda i: (jnp.minimum((i + 1) * hb, nhb - 1), 0)),
            pl.BlockSpec((None, kw, cc), cst),
            pl.BlockSpec((None, 1, cc), cst),
            pl.BlockSpec((None, 1, cc), cst),
            pl.BlockSpec((None, 1, cc), cst),
            pl.BlockSpec((None, 1, cc), lambda i: (l, 0, 0)),
        ],
        out_specs=pl.BlockSpec((tm, cc), lambda i: (i, 0)),
        scratch_shapes=[pltpu.VMEM((tm + 2 * HALO, cc), F32),
                        pltpu.VMEM((SUBLANES - 1, nsh, cc), F32),
                        pltpu.VMEM((tm, cc), F32)],
        compiler_params=_cparams("parallel"),
        name="conv",
    )(glu, glu, glu, w, b, lng, lnb, gg)


def _attn_stream(qt, k_ref, vt_ref, *, tk, nk, exact):
    tq = qt.shape[1]
    ones = jnp.ones((ONES_ROWS, tk), BF)
    score = lambda c: _dot(k_ref[c * tk:(c + 1) * tk, :], qt)
    scores = [score(c) for c in range(min(AHEAD, nk))]
    run_max = []
    excess = jnp.zeros((1, tq), F32)
    acc = m_acc = None
    for c in range(nk):
        if c + AHEAD < nk:
            scores.append(score(c + AHEAD))
        st = scores.pop(0)
        cmax = jnp.max(st, axis=0, keepdims=True)
        run_max.append(cmax if c == 0 else jnp.maximum(run_max[-1], cmax))
        if c < exact:
            m_use = run_max[c]
        else:
            m_use = run_max[max(c - LAG, exact - 1)]
            excess = jnp.maximum(excess, cmax - m_use)
        p = jnp.exp2(st - m_use).astype(BF)
        vt_ext = jnp.concatenate([vt_ref[:, c * tk:(c + 1) * tk], ones], axis=0)
        pv = _dot(vt_ext, p)
        acc = pv if c == 0 else jnp.exp2(m_acc - m_use) * acc + pv
        m_acc = m_use
    return acc, excess


def _attn_body(qt_ref, k_ref, vt_ref, o_ref, *, tk, nk):
    qt = qt_ref[...]

    def finish(acc):
        o_ref[...] = (acc[:VDIM] / acc[VDIM:VDIM + 1]).T

    acc, excess = _attn_stream(qt, k_ref, vt_ref, tk=tk, nk=nk, exact=1)
    finish(acc)

    @pl.when(jnp.max(excess) > MAX_EXCESS)
    def _():
        finish(_attn_stream(qt, k_ref, vt_ref, tk=tk, nk=nk, exact=nk)[0])


def _attn(qt, k, vt, *, seq, heads, tq, tk):
    t = k.shape[0]
    bsz = t // seq
    qpt = seq // tq
    body = functools.partial(_attn_body, tk=tk, nk=seq // tk)
    return pl.pallas_call(
        body,
        out_shape=jax.ShapeDtypeStruct((t, heads * VDIM), F32),
        grid=(bsz, heads, qpt),
        in_specs=[
            pl.BlockSpec((HEAD_W, tq), lambda b, h, i: (h, b * qpt + i)),
            pl.BlockSpec((seq, HEAD_W), lambda b, h, i: (b, h)),
            pl.BlockSpec((VDIM, seq), lambda b, h, i: (h, b)),
        ],
        out_specs=pl.BlockSpec((tq, VDIM), lambda b, h, i: (b * qpt + i, h)),
        compiler_params=_cparams("parallel", "parallel", "arbitrary"),
        name="mla_attn",
    )(qt, k, vt)


def _proj_body(a_ref, b_ref, gb_ref, w_ref, gpost_ref, x_ref, o_ref, *, norm_b):
    half = a_ref.shape[-1]
    b = b_ref[...]
    if norm_b:
        b = (_rms(b) * gb_ref[...]).astype(BF)
    t = _dot(a_ref[...], w_ref[:half, :]) + _dot(b, w_ref[half:, :])
    o_ref[...] = x_ref[...] + _rms(t) * gpost_ref[...]


def _proj(a, b, a_col, b_col, l, gb, gb_col, w, gpost, x, *, tm, norm_b):
    t, d = x.shape
    half = w.shape[1] // 2
    body = functools.partial(_proj_body, norm_b=norm_b)
    cst = lambda i: (l, 0, 0)
    return pl.pallas_call(
        body,
        out_shape=jax.ShapeDtypeStruct((t, d), F32),
        grid=(t // tm,),
        in_specs=[
            pl.BlockSpec((tm, half), lambda i: (i, a_col)),
            pl.BlockSpec((tm, half), lambda i: (i, b_col)),
            pl.BlockSpec((None, 1, half), lambda i: (l, 0, gb_col)),
            _const_spec((None, 2 * half, d), cst),
            pl.BlockSpec((None, 1, d), cst),
            pl.BlockSpec((tm, d), lambda i: (i, 0)),
        ],
        out_specs=pl.BlockSpec((tm, d), lambda i: (i, 0)),
        compiler_params=_cparams("parallel"),
        name="proj",
    )(a, b, gb, w, gpost, x)


def _mem_kv_body(m_ref, g_ref, wk_ref, wv_ref, k_ref, v_ref):
    m = (_rms(m_ref[...]) * g_ref[...]).astype(BF)
    k_ref[...] = _dot(m, wk_ref[...]).astype(BF)
    v_ref[...] = _dot(m, wv_ref[...]).astype(BF)


def _mem_kv(mem, g, wk, wv, *, tn):
    rows, d = mem.shape
    depth = wk.shape[0]
    return pl.pallas_call(
        _mem_kv_body,
        out_shape=(jax.ShapeDtypeStruct((depth, rows, d), BF),
                   jax.ShapeDtypeStruct((depth, rows, d), BF)),
        grid=(depth, d // tn),
        in_specs=[
            pl.BlockSpec((rows, d), lambda l, n: (0, 0)),
            pl.BlockSpec((None, 1, d), lambda l, n: (l, 0, 0)),
            pl.BlockSpec((None, d, tn), lambda l, n: (l, 0, n)),
            pl.BlockSpec((None, d, tn), lambda l, n: (l, 0, n)),
        ],
        out_specs=(pl.BlockSpec((None, rows, tn), lambda l, n: (l, 0, n)),
                   pl.BlockSpec((None, rows, tn), lambda l, n: (l, 0, n))),
        compiler_params=_cparams("parallel", "parallel"),
        name="mem_kv",
    )(mem, g, wk, wv)


def _xattn_body(x_ref, g_ref, wq_ref, k_ref, v_ref, wo_ref, gpost_ref, o_ref, *, scale):
    x = x_ref[...]
    h = (_rms(x) * g_ref[...]).astype(BF)
    hd = x_ref.shape[-1] // X_HEADS
    cols = [slice(hh * hd, (hh + 1) * hd) for hh in range(X_HEADS)]
    qproj = lambda hh: _dot(h, wq_ref[:, cols[hh]]).astype(BF)
    q_next = qproj(0)
    heads_out = []
    for hh in range(X_HEADS):
        cs = cols[hh]
        s = _dot_t(q_next, k_ref[:, cs]) * scale
        if hh + 1 < X_HEADS:
            q_next = qproj(hh + 1)
        e = jnp.exp(s - jnp.max(s, axis=-1, keepdims=True))
        p = e * (1.0 / jnp.sum(e, axis=-1, keepdims=True))
        heads_out.append(_dot(p.astype(BF), v_ref[:, cs]).astype(BF))
    t = _dot(jnp.concatenate(heads_out, axis=-1), wo_ref[...])
    o_ref[...] = x + _rms(t) * gpost_ref[...]


def _xattn(x, l, g, wq, kmem, vmem, wo, gpost, *, seq, n_mem, tm):
    t, d = x.shape
    spt = seq // tm
    body = functools.partial(_xattn_body, scale=float((d // X_HEADS) ** -0.5))
    cst = lambda i: (l, 0, 0)
    return pl.pallas_call(
        body,
        out_shape=jax.ShapeDtypeStruct((t, d), F32),
        grid=(t // tm,),
        in_specs=[
            pl.BlockSpec((tm, d), lambda i: (i, 0)),
            pl.BlockSpec((None, 1, d), cst),
            _const_spec((None, d, d), cst),
            pl.BlockSpec((None, n_mem, d), lambda i: (l, i // spt, 0)),
            pl.BlockSpec((None, n_mem, d), lambda i: (l, i // spt, 0)),
            _const_spec((None, d, d), cst),
            pl.BlockSpec((None, 1, d), cst),
        ],
        out_specs=pl.BlockSpec((tm, d), lambda i: (i, 0)),
        compiler_params=_cparams("parallel"),
        name="xattn",
    )(x, g, wq, kmem, vmem, wo, gpost)


def _rope_tables(seq):
    inv = 1.0 / (ROPE_THETA ** (jnp.arange(0, ROPE, 2, dtype=F32) / ROPE))
    ang = jnp.arange(seq, dtype=F32)[:, None] * inv[None, :]
    ang = jnp.concatenate([ang, ang], axis=-1)
    pad = jnp.zeros((seq, 128 - ROPE), F32)
    return (jnp.concatenate([jnp.cos(ang), pad], axis=-1),
            jnp.concatenate([jnp.sin(ang), pad], axis=-1))


def _rot_cols(w):
    half = w.shape[-1] // 2
    return jnp.concatenate([-w[..., half:], w[..., :half]], axis=-1)


def _tile(n, want):
    t = min(n, want)
    while n % t:
        t //= 2
    return t


def _trunk(x3, mem3, p, depth, heads, cc):
    bsz, seq, d = x3.shape
    n_mem = mem3.shape[1]
    x = x3.reshape(bsz * seq, d)
    mem = mem3.reshape(bsz * n_mem, d)
    cosp, sinp = _rope_tables(seq)
    kmem, vmem = _mem_kv(mem, p['g_mem'], p['w_xk'], p['w_xv'], tn=_tile(d, 512))
    tm = _tile(seq, 512)
    tm_ffn = _tile(seq, 1024)
    tm_mix = _tile(seq, 512)
    tm_conv = _tile(seq, 512)
    dff = p['w_ffn1_gate'].shape[-1]
    tf = _tile(dff, 512)
    for l in range(depth):
        x = _ffn(x, l, p['g_ffn1_pre'], p['w_ffn1_gate'], p['w_ffn1_up'], p['w_ffn1_down'],
                 p['g_ffn1_post'], tm=tm_ffn, tf=tf)
        glu, q, k, v = _mix_in(x, l, p['g_mix_pre'], p['w_in_ext'], p['g_q_lat'], p['g_kv_lat'],
                               p['w_qm'], p['w_qr'], p['w_k'], p['w_v'], cosp, sinp,
                               seq=seq, tm=tm_mix, cc=cc, heads=heads)
        conv_n = _conv(glu, l, p['conv_dw_w'], p['conv_dw_b'], p['conv_ln_g'], p['conv_ln_b'],
                       p['g_group_out'], seq=seq, tm=tm_conv, rc=_tile(tm_conv // ROW_GROUP, 64))
        attn = _attn(q, k, v, seq=seq, heads=heads, tq=_tile(seq, 1024), tk=_tile(seq, 512))
        x = _proj(conv_n, attn, 0, 0, l, p['g_group_out'], 1, p['w_out'], p['g_mix_post'], x,
                  tm=tm, norm_b=True)
        x = _xattn(x, l, p['g_x_pre'], p['w_xq'], kmem, vmem, p['w_xo'], p['g_x_post'],
                   seq=seq, n_mem=n_mem, tm=tm)
        x = _ffn(x, l, p['g_ffn2_pre'], p['w_ffn2_gate'], p['w_ffn2_up'], p['w_ffn2_down'],
                 p['g_ffn2_post'], tm=tm_ffn, tf=tf)
    return x.reshape(bsz, seq, d)


def kernel(x_prompt, x_sample, mem_prompt, mem_sample, g_ffn1_pre, g_ffn1_post, w_ffn1_gate, w_ffn1_up, w_ffn1_down, g_mix_pre, g_mix_post, w_in, conv_dw_w, conv_dw_b, conv_ln_g, conv_ln_b, g_q_lat, g_kv_lat, w_q_up, w_kv_up, g_group_out, w_out, g_x_pre, g_x_post, g_mem, w_xq, w_xk, w_xv, w_xo, g_ffn2_pre, g_ffn2_post, w_ffn2_gate, w_ffn2_up, w_ffn2_down):
    depth, d, _ = w_in.shape
    cc = conv_dw_w.shape[-1]
    ql, kvl = g_q_lat.shape[-1], g_kv_lat.shape[-1]
    heads = w_kv_up.shape[-1] // (NOPE + VDIM)
    o3 = 2 * cc + ql + kvl

    row = lambda g: g[:, None, :]
    p = dict(
        g_ffn1_pre=row(g_ffn1_pre), g_ffn1_post=row(g_ffn1_post),
        g_ffn2_pre=row(g_ffn2_pre), g_ffn2_post=row(g_ffn2_post),
        g_mix_pre=row(g_mix_pre), g_mix_post=row(g_mix_post),
        g_x_pre=row(g_x_pre), g_x_post=row(g_x_post), g_mem=row(g_mem),
        g_q_lat=row(g_q_lat), g_kv_lat=row(g_kv_lat), g_group_out=row(g_group_out),
        conv_dw_w=conv_dw_w, conv_dw_b=row(conv_dw_b), conv_ln_g=row(conv_ln_g),
        conv_ln_b=row(conv_ln_b),
        w_ffn1_gate=w_ffn1_gate.astype(BF), w_ffn1_up=w_ffn1_up.astype(BF),
        w_ffn1_down=w_ffn1_down.astype(BF),
        w_ffn2_gate=w_ffn2_gate.astype(BF), w_ffn2_up=w_ffn2_up.astype(BF),
        w_ffn2_down=w_ffn2_down.astype(BF),
        w_out=w_out.astype(BF), w_xq=w_xq.astype(BF), w_xk=w_xk.astype(BF),
        w_xv=w_xv.astype(BF), w_xo=w_xo.astype(BF),
    )
    w_kr = w_in[:, :, o3:]
    z64 = jnp.zeros((depth, d, 128 - ROPE), w_in.dtype)
    p['w_in_ext'] = jnp.concatenate([w_in[:, :, :o3], w_kr, z64, _rot_cols(w_kr), z64],
                                    axis=-1).astype(BF)
    wq4 = w_q_up.reshape(depth, ql, heads, NOPE + ROPE)
    wq_nope, wq_rope = wq4[..., :NOPE], wq4[..., NOPE:]
    zq = jnp.zeros((depth, ql, heads, 128 - ROPE), w_q_up.dtype)
    p['w_qm'] = jnp.concatenate([wq_rope, zq, wq_nope], axis=-1).reshape(
        depth, ql, heads * HEAD_W).astype(BF)
    p['w_qr'] = jnp.concatenate([_rot_cols(wq_rope), zq], axis=-1).reshape(
        depth, ql, heads * 128).astype(BF)
    wkv4 = w_kv_up.reshape(depth, kvl, heads, NOPE + VDIM)
    p['w_k'] = wkv4[..., :NOPE].reshape(depth, kvl, heads * NOPE).astype(BF)
    p['w_v'] = wkv4[..., NOPE:].reshape(depth, kvl, heads * VDIM).astype(BF)

    y_prompt = _trunk(x_prompt, mem_prompt, p, depth, heads, cc)
    y_sample = _trunk(x_sample, mem_sample, p, depth, heads, cc)
    return (y_prompt, y_sample)
```
